```python
import math
import jax
import jax.numpy as jnp
from jax import lax
import numpy as np


D_MODEL = 2048
BATCH = 8
SEQ = 2048
DEPTH = 1

MEM_LEN = 256
MIX_WIDTH = D_MODEL
MOBA_WIDTH = MIX_WIDTH // 2
MOBA_HEADS = 8
MOBA_HEAD_DIM = MOBA_WIDTH // MOBA_HEADS
MOBA_BLOCK = 256
MOBA_TOPK = 3
MOBA_Q_CHUNK = 16
GLA_WIDTH = MIX_WIDTH - MOBA_WIDTH
GLA_HEADS = 4
GLA_KEY_DIM = GLA_WIDTH // (2 * GLA_HEADS)
GLA_VAL_DIM = GLA_WIDTH // GLA_HEADS
GLA_GATE_RANK = 16
GLA_GATE_TAU = 16.0
GLA_CHUNK = 64
XATTN_HEADS = 4
XATTN_HEAD_DIM = D_MODEL // XATTN_HEADS
PEER_HEADS = 8
PEER_N_KEYS = 128
PEER_N_EXPERTS = PEER_N_KEYS * PEER_N_KEYS
PEER_QUERY_DIM = 256
PEER_TOPK = 16
PEER_TOKEN_CHUNK = 128
DEEPNORM_ALPHA = (2.0 * DEPTH) ** 0.25
DEEPNORM_BETA = (8.0 * DEPTH) ** -0.25
LN_EPS = 1e-5
RMS_EPS = 1e-6
NEG_INF = -1e30
W_IN_SIZES = (MOBA_WIDTH, MOBA_WIDTH, MOBA_WIDTH, GLA_HEADS * GLA_KEY_DIM, GLA_HEADS * GLA_KEY_DIM, GLA_WIDTH, GLA_WIDTH, GLA_GATE_RANK)
W_IN_COLS = 3 * MOBA_WIDTH + 2 * GLA_HEADS * GLA_KEY_DIM + 2 * GLA_WIDTH + GLA_GATE_RANK

kernel_name = 'hybrid_moba_gla_peer_deepnorm'


def _split_heads(a, n_heads):
    b, s, _ = a.shape
    return a.reshape(b, s, n_heads, -1).transpose(0, 2, 1, 3)


def _merge_heads(a):
    b, h, s, d = a.shape
    return a.transpose(0, 2, 1, 3).reshape(b, s, h * d)


def layer_norm(x, g, b):
    xf = x.astype(jnp.float32)
    mu = jnp.mean(xf, axis=-1, keepdims=True)
    var = jnp.mean(jnp.square(xf - mu), axis=-1, keepdims=True)
    y = (xf - mu) * lax.rsqrt(var + LN_EPS) * g.astype(jnp.float32) + b.astype(jnp.float32)
    return y.astype(x.dtype)


def moba_attention(q, k, v):
    B, H, S, Dh = q.shape
    nb = -(-S // MOBA_BLOCK)
    s_pad = nb * MOBA_BLOCK
    pad = ((0, 0), (0, 0), (0, s_pad - S), (0, 0))
    kb = jnp.pad(k, pad).reshape(B, H, nb, MOBA_BLOCK, Dh)
    vb = jnp.pad(v, pad).reshape(B, H, nb, MOBA_BLOCK, Dh)
    k_mean = jnp.mean(kb.astype(jnp.float32), axis=3)
    gate = jnp.einsum('bhsd,bhnd->bhsn', q.astype(jnp.float32), k_mean)
    q_block = jnp.arange(S) // MOBA_BLOCK
    past = jnp.arange(nb)[None, :] < q_block[:, None]
    gate = jnp.where(past, gate, NEG_INF)
    n_sel = min(MOBA_TOPK, nb)
    sel_score, sel_idx = lax.top_k(gate, n_sel)
    sel_valid = sel_score > 0.5 * NEG_INF

    nc = S // MOBA_Q_CHUNK

    def to_chunks(a):
        a = a.reshape(B, H, nc, MOBA_Q_CHUNK, *a.shape[3:])
        return jnp.moveaxis(a, 2, 0)

    starts = jnp.arange(nc) * MOBA_Q_CHUNK
    b_idx = jnp.arange(B)[:, None, None, None]
    h_idx = jnp.arange(H)[None, :, None, None]
    scale = Dh ** -0.5
    key_off = jnp.arange(MOBA_BLOCK)
    q_off = jnp.arange(MOBA_Q_CHUNK)

    def attend(args):
        qc, idx, valid, start = args
        k_sel = kb[b_idx, h_idx, idx]
        v_sel = vb[b_idx, h_idx, idx]
        own = start // MOBA_BLOCK
        k_own = lax.dynamic_index_in_dim(kb, own, axis=2, keepdims=False)
        v_own = lax.dynamic_index_in_dim(vb, own, axis=2, keepdims=False)
        s_sel = jnp.einsum('bhqd,bhqtkd->bhqtk', qc, k_sel).astype(jnp.float32) * scale
        s_sel = jnp.where(valid[..., None], s_sel, NEG_INF)
        s_own = jnp.einsum('bhqd,bhkd->bhqk', qc, k_own).astype(jnp.float32) * scale
        causal = (own * MOBA_BLOCK + key_off)[None, :] <= (start + q_off)[:, None]
        s_own = jnp.where(causal, s_own, NEG_INF)
        scores = jnp.concatenate([s_sel.reshape(B, H, MOBA_Q_CHUNK, -1), s_own], axis=-1)
        p = jax.nn.softmax(scores, axis=-1).astype(v.dtype)
        p_sel = p[..., : n_sel * MOBA_BLOCK].reshape(B, H, MOBA_Q_CHUNK, n_sel, MOBA_BLOCK)
        p_own = p[..., n_sel * MOBA_BLOCK:]
        return (jnp.einsum('bhqtk,bhqtkd->bhqd', p_sel, v_sel)
                + jnp.einsum('bhqk,bhkd->bhqd', p_own, v_own))

    out = lax.map(attend, (to_chunks(q), to_chunks(sel_idx), to_chunks(sel_valid), starts))
    return jnp.moveaxis(out, 0, 2).reshape(B, H, S, Dh)


def gla_attention(q, k, v, log_a):
    B, H, S, Dk = q.shape
    Dv = v.shape[-1]
    n = S // GLA_CHUNK
    f32 = jnp.float32
    qf = q.astype(f32).reshape(B, H, n, GLA_CHUNK, Dk) * Dk ** -0.5
    kf = k.astype(f32).reshape(B, H, n, GLA_CHUNK, Dk)
    vf = v.astype(f32).reshape(B, H, n, GLA_CHUNK, Dv)
    cum = jnp.cumsum(log_a.astype(f32).reshape(B, H, n, GLA_CHUNK, Dk), axis=3)
    cum_last = cum[..., -1:, :]
    q_dec = qf * jnp.exp(cum)
    k_dec = kf * jnp.exp(-cum)
    k_state = kf * jnp.exp(cum_last - cum)
    causal = jnp.tril(jnp.ones((GLA_CHUNK, GLA_CHUNK), dtype=bool))
    a_intra = jnp.where(causal, jnp.einsum('bhnid,bhnjd->bhnij', q_dec, k_dec), 0.0)
    o_intra = jnp.einsum('bhnij,bhnje->bhnie', a_intra, vf)
    delta = jnp.einsum('bhncd,bhnce->bhnde', k_state, vf)
    decay = jnp.exp(cum_last[..., 0, :])

    def step(state, inp):
        dec, dlt = inp
        return state * dec[..., None] + dlt, state

    _, states = lax.scan(step, jnp.zeros((B, H, Dk, Dv), f32),
                         (jnp.moveaxis(decay, 2, 0), jnp.moveaxis(delta, 2, 0)))
    states = jnp.moveaxis(states, 0, 2)
    o_inter = jnp.einsum('bhncd,bhnde->bhnce', q_dec, states)
    return (o_intra + o_inter).reshape(B, H, S, Dv)


def hybrid_mixer(x, w_in, gla_gate_up, gla_gate_bias, gla_norm_g, w_out):
    proj = x @ w_in
    cuts = np.cumsum(W_IN_SIZES)[:-1].tolist()
    q_m, k_m, v_m, q_g, k_g, v_g, r_g, gate_lr = jnp.split(proj, cuts, axis=-1)
    o_moba = moba_attention(_split_heads(q_m, MOBA_HEADS), _split_heads(k_m, MOBA_HEADS),
                            _split_heads(v_m, MOBA_HEADS))
    o_moba = _merge_heads(o_moba)
    log_a = jax.nn.log_sigmoid((gate_lr @ gla_gate_up + gla_gate_bias).astype(jnp.float32)) / GLA_GATE_TAU
    o_gla = gla_attention(_split_heads(q_g, GLA_HEADS), _split_heads(k_g, GLA_HEADS),
                          _split_heads(v_g, GLA_HEADS), _split_heads(log_a, GLA_HEADS))
    o_gla = o_gla * lax.rsqrt(jnp.mean(jnp.square(o_gla), axis=-1, keepdims=True) + RMS_EPS)
    o_gla = o_gla * gla_norm_g.astype(jnp.float32)
    o_gla = (_merge_heads(o_gla) * jax.nn.silu(r_g.astype(jnp.float32))).astype(x.dtype)
    return jnp.concatenate([o_moba, o_gla], axis=-1) @ w_out


def memory_cross_attention(x, mem, w_q, w_kv, w_o):
    q = _split_heads(x @ w_q, XATTN_HEADS)
    k, v = jnp.split(mem @ w_kv, 2, axis=-1)
    k = _split_heads(k, XATTN_HEADS)
    v = _split_heads(v, XATTN_HEADS)
    s = jnp.einsum('bhsd,bhmd->bhsm', q, k).astype(jnp.float32) * XATTN_HEAD_DIM ** -0.5
    p = jax.nn.softmax(s, axis=-1).astype(v.dtype)
    o = jnp.einsum('bhsm,bhmd->bhsd', p, v)
    return _merge_heads(o) @ w_o


def peer_ffn(x, w_query, sub_keys, expert_down, expert_up):
    B, S, D = x.shape
    T = B * S
    xt = x.reshape(T, D)
    q = (xt @ w_query).reshape(T, PEER_HEADS, 2, PEER_QUERY_DIM // 2)
    s = jnp.einsum('thpd,hpnd->thpn', q, sub_keys).astype(jnp.float32)
    s_top, i_top = lax.top_k(s, PEER_TOPK)
    cand = s_top[:, :, 0, :, None] + s_top[:, :, 1, None, :]
    cand_idx = i_top[:, :, 0, :, None] * PEER_N_KEYS + i_top[:, :, 1, None, :]
    cand = cand.reshape(T, PEER_HEADS, PEER_TOPK * PEER_TOPK)
    cand_idx = cand_idx.reshape(T, PEER_HEADS, PEER_TOPK * PEER_TOPK)
    best, pos = lax.top_k(cand, PEER_TOPK)
    expert_idx = jnp.take_along_axis(cand_idx, pos, axis=-1)
    gates = jax.nn.softmax(best, axis=-1)
    nc = T // PEER_TOKEN_CHUNK

    def apply(args):
        xc, idx, g = args
        u = expert_down[idx]
        h = jax.nn.gelu(jnp.einsum('td,thkd->thk', xc, u).astype(jnp.float32), approximate=False)
        vsel = expert_up[idx]
        return jnp.einsum('thk,thkd->td', (g * h).astype(x.dtype), vsel)

    out = lax.map(apply, (xt.reshape(nc, PEER_TOKEN_CHUNK, D),
                          expert_idx.reshape(nc, PEER_TOKEN_CHUNK, PEER_HEADS, PEER_TOPK),
                          gates.reshape(nc, PEER_TOKEN_CHUNK, PEER_HEADS, PEER_TOPK)))
    return out.reshape(B, S, D)


def setup_inputs(seed: int = 0) -> dict:
    key = jax.random.key(seed)
    ks = jax.random.split(key, 20)
    L = DEPTH
    d = D_MODEL

    def nrm(k, shape, scale):
        return jax.random.normal(k, shape, jnp.float32) * scale

    return {
        'x': nrm(ks[0], (BATCH, SEQ, d), 1.0),
        'mem': nrm(ks[1], (BATCH, MEM_LEN, d), 1.0),
        'w_in': nrm(ks[2], (L, d, W_IN_COLS), d ** -0.5),
        'gla_gate_up': nrm(ks[3], (L, GLA_GATE_RANK, GLA_HEADS * GLA_KEY_DIM), GLA_GATE_RANK ** -0.5),
        'gla_gate_bias': nrm(ks[4], (L, GLA_HEADS * GLA_KEY_DIM), 0.1),
        'gla_norm_g': 1.0 + nrm(ks[5], (L, GLA_VAL_DIM), 0.02),
        'w_out': nrm(ks[6], (L, MIX_WIDTH, d), DEEPNORM_BETA * MIX_WIDTH ** -0.5),
        'ln1_g': 1.0 + nrm(ks[7], (L, d), 0.02),
        'ln1_b': nrm(ks[8], (L, d), 0.02),
        'xattn_wq': nrm(ks[9], (L, d, d), d ** -0.5),
        'xattn_wkv': nrm(ks[10], (L, d, 2 * d), d ** -0.5),
        'xattn_wo': nrm(ks[11], (L, d, d), DEEPNORM_BETA * d ** -0.5),
        'ln2_g': 1.0 + nrm(ks[12], (L, d), 0.02),
        'ln2_b': nrm(ks[13], (L, d), 0.02),
        'peer_wq': nrm(ks[14], (L, d, PEER_HEADS * PEER_QUERY_DIM), d ** -0.5),
        'peer_sub_keys': nrm(ks[15], (L, PEER_HEADS, 2, PEER_N_KEYS, PEER_QUERY_DIM // 2), (PEER_QUERY_DIM // 2) ** -0.5),
        'peer_u': nrm(ks[16], (L, PEER_N_EXPERTS, d), d ** -0.5),
        'peer_v': nrm(ks[17], (L, PEER_N_EXPERTS, d), DEEPNORM_BETA * PEER_HEADS ** -0.5),
        'ln3_g': 1.0 + nrm(ks[18], (L, d), 0.02),
        'ln3_b': nrm(ks[19], (L, d), 0.02),
    }


def reference(x, mem, w_in, gla_gate_up, gla_gate_bias, gla_norm_g, w_out, ln1_g, ln1_b,
              xattn_wq, xattn_wkv, xattn_wo, ln2_g, ln2_b,
              peer_wq, peer_sub_keys, peer_u, peer_v, ln3_g, ln3_b):
    h = x
    for l in range(DEPTH):
        mix = hybrid_mixer(h, w_in[l], gla_gate_up[l], gla_gate_bias[l], gla_norm_g[l], w_out[l])
        h = layer_norm(DEEPNORM_ALPHA * h + mix, ln1_g[l], ln1_b[l])
        xa = memory_cross_attention(h, mem, xattn_wq[l], xattn_wkv[l], xattn_wo[l])
        h = layer_norm(DEEPNORM_ALPHA * h + xa, ln2_g[l], ln2_b[l])
        ff = peer_ffn(h, peer_wq[l], peer_sub_keys[l], peer_u[l], peer_v[l])
        h = layer_norm(DEEPNORM_ALPHA * h + ff, ln3_g[l], ln3_b[l])
    return h
```

```python
import functools
import math

import jax
import jax.numpy as jnp
import numpy as np
from jax import lax
from jax.experimental import pallas as pl
from jax.experimental.pallas import tpu as pltpu

MOBA_HEADS = 8
MOBA_HEAD_DIM = 128
MOBA_BLOCK = 256
MOBA_TOPK = 3
GLA_HEADS = 4
GLA_KEY_DIM = 128
GLA_VAL_DIM = 256
GLA_GATE_RANK = 16
GLA_GATE_TAU = 16.0
GLA_CHUNK = 64
XATTN_HEADS = 4
PEER_HEADS = 8
PEER_N_KEYS = 128
PEER_TOPK = 16
DEPTH = 1
DEEPNORM_ALPHA = (2.0 * DEPTH) ** 0.25
LN_EPS = 1e-5
RMS_EPS = 1e-6
NEG_INF = -1e30

V7X_LANES = 128
V7X_VMEM_LIMIT_BYTES = 56 * 1024 * 1024

_BF16 = jnp.bfloat16
_F32 = jnp.float32
_NT = (((1,), (1,)), ((), ()))
_TN = (((0,), (0,)), ((), ()))


def _params(*semantics):
    return pltpu.CompilerParams(dimension_semantics=semantics,
                                vmem_limit_bytes=V7X_VMEM_LIMIT_BYTES)


def _matmul_kernel(a_ref, b_ref, o_ref):
    o_ref[...] = jnp.dot(a_ref[...], b_ref[...],
                         preferred_element_type=_F32).astype(o_ref.dtype)


def _matmul(a, b, *, bm, bn, out_dtype):
    m, k = a.shape
    _, n = b.shape
    bm = min(bm, m)
    assert m % bm == 0 and n % bn == 0
    return pl.pallas_call(
        _matmul_kernel,
        grid=(m // bm, n // bn),
        in_specs=[pl.BlockSpec((bm, k), lambda i, j: (i, 0)),
                  pl.BlockSpec((k, bn), lambda i, j: (0, j))],
        out_specs=pl.BlockSpec((bm, bn), lambda i, j: (i, j)),
        out_shape=jax.ShapeDtypeStruct((m, n), out_dtype),
        compiler_params=_params("parallel", "parallel"),
        name="matmul",
    )(a, b)


def _layer_norm_rows(z, g, b):
    mu = jnp.mean(z, axis=-1, keepdims=True)
    zc = z - mu
    var = jnp.mean(zc * zc, axis=-1, keepdims=True)
    return zc * lax.rsqrt(var + LN_EPS) * g + b


def _matmul_residual_ln_kernel(a1_ref, a2_ref, w_ref, r_ref, g_ref, b_ref, o_ref, obf_ref):
    kh = a1_ref.shape[1]
    acc = jnp.dot(a1_ref[...], w_ref[:kh, :], preferred_element_type=_F32)
    acc = acc + jnp.dot(a2_ref[...], w_ref[kh:, :], preferred_element_type=_F32)
    y = _layer_norm_rows(DEEPNORM_ALPHA * r_ref[...] + acc, g_ref[...], b_ref[...])
    o_ref[...] = y
    obf_ref[...] = y.astype(_BF16)


def _matmul_residual_ln(a1, a1_col, a2, a2_col, w, resid, g, b, *, bm):
    m, d = resid.shape
    k = w.shape[0]
    kh = k // 2
    return pl.pallas_call(
        _matmul_residual_ln_kernel,
        grid=(m // bm,),
        in_specs=[pl.BlockSpec((bm, kh), lambda i: (i, a1_col)),
                  pl.BlockSpec((bm, kh), lambda i: (i, a2_col)),
                  pl.BlockSpec((k, d), lambda i: (0, 0)),
                  pl.BlockSpec((bm, d), lambda i: (i, 0)),
                  pl.BlockSpec((1, d), lambda i: (0, 0)),
                  pl.BlockSpec((1, d), lambda i: (0, 0))],
        out_specs=[pl.BlockSpec((bm, d), lambda i: (i, 0)),
                   pl.BlockSpec((bm, d), lambda i: (i, 0))],
        out_shape=[jax.ShapeDtypeStruct((m, d), _F32),
                   jax.ShapeDtypeStruct((m, d), _BF16)],
        compiler_params=_params("parallel"),
        name="matmul_residual_ln",
    )(a1, a2, w, resid, g, b)


def _moba_kernel(q_ref, k_ref, v_ref, o_ref, *, n_blocks):
    qi = pl.program_id(2)
    bs = MOBA_BLOCK
    q = q_ref[...]
    k = k_ref[...]
    dh = q.shape[1]
    kmean = jnp.sum(k.reshape(n_blocks, bs, dh), axis=1) * (1.0 / bs)
    kmean = jnp.concatenate(
        [kmean, jnp.zeros((V7X_LANES - n_blocks, dh), _F32)], axis=0)
    gate = lax.dot_general(q, kmean, _NT, precision=lax.Precision.HIGHEST,
                           preferred_element_type=_F32)
    lane = lax.broadcasted_iota(jnp.int32, gate.shape, 1)
    gate = jnp.where(lane < qi, gate, NEG_INF)
    sel = jnp.zeros(gate.shape, _F32)
    for _ in range(min(MOBA_TOPK, n_blocks)):
        m = jnp.max(gate, axis=-1, keepdims=True)
        first = jnp.min(jnp.where(gate == m, lane, V7X_LANES), axis=-1, keepdims=True)
        hit = lane == first
        sel = jnp.where(jnp.logical_and(hit, m > 0.5 * NEG_INF), 1.0, sel)
        gate = jnp.where(hit, -jnp.inf, gate)

    qb = q.astype(_BF16)
    kb = k.astype(_BF16)
    vb = v_ref[...].astype(_BF16)
    scale = dh ** -0.5
    row = lax.broadcasted_iota(jnp.int32, (bs, bs), 0)
    col = lax.broadcasted_iota(jnp.int32, (bs, bs), 1)
    causal = (col <= row).astype(_F32)
    parts = []
    for j in range(n_blocks):
        s = lax.dot_general(qb, kb[j * bs:(j + 1) * bs, :], _NT,
                            preferred_element_type=_F32) * scale
        own = jnp.where(qi == j, 1.0, 0.0)
        allowed = (sel[:, j:j + 1] + own * causal) > 0.0
        parts.append(jnp.where(allowed, s, NEG_INF))
    s_all = jnp.concatenate(parts, axis=-1)
    m = jnp.max(s_all, axis=-1, keepdims=True)
    p = jnp.exp(s_all - m)
    l = jnp.sum(p, axis=-1, keepdims=True)
    o = jnp.dot(p.astype(_BF16), vb, preferred_element_type=_F32)
    o_ref[...] = (o / l).astype(o_ref.dtype)


def _moba(proj, *, batch, seq, q_col, k_col, v_col):
    nb = seq // MOBA_BLOCK
    dh = MOBA_HEAD_DIM
    return pl.pallas_call(
        functools.partial(_moba_kernel, n_blocks=nb),
        grid=(batch, MOBA_HEADS, nb),
        in_specs=[pl.BlockSpec((MOBA_BLOCK, dh), lambda b, h, i: (b * nb + i, q_col + h)),
                  pl.BlockSpec((seq, dh), lambda b, h, i: (b, k_col + h)),
                  pl.BlockSpec((seq, dh), lambda b, h, i: (b, v_col + h))],
        out_specs=pl.BlockSpec((MOBA_BLOCK, dh), lambda b, h, i: (b * nb + i, h)),
        out_shape=jax.ShapeDtypeStruct((batch * seq, MOBA_HEADS * dh), _BF16),
        compiler_params=_params("parallel", "parallel", "arbitrary"),
        name="moba",
    )(proj, proj, proj)


def _log_sigmoid(z):
    return jnp.minimum(z, 0.0) - jnp.log(1.0 + jnp.exp(-jnp.abs(z)))


def _gla_kernel(q_ref, k_ref, v_ref, r_ref, lr_ref, up_ref, bias_ref, g_ref, o_ref,
                state_ref, *, n_chunks):
    c = GLA_CHUNK
    dk = q_ref.shape[1]
    state_ref[...] = jnp.zeros_like(state_ref)
    row = lax.broadcasted_iota(jnp.int32, (c, c), 0)
    col = lax.broadcasted_iota(jnp.int32, (c, c), 1)
    tril = col <= row
    tril_f = tril.astype(_F32)
    up = up_ref[...]
    bias = bias_ref[...]
    gain = g_ref[...]
    hi = lax.Precision.HIGHEST

    def chunk(n, carry):
        r0 = pl.multiple_of(n * c, c)
        rows = pl.ds(r0, c)
        z = jnp.dot(lr_ref[rows, :], up, precision=hi, preferred_element_type=_F32) + bias
        log_a = _log_sigmoid(z) * (1.0 / GLA_GATE_TAU)
        cum = jnp.dot(tril_f, log_a, precision=hi, preferred_element_type=_F32)
        cum_last = cum[c - 1:c, :]
        q = q_ref[rows, :] * (dk ** -0.5)
        k = k_ref[rows, :]
        v = v_ref[rows, :].astype(_BF16)
        q_dec = (q * jnp.exp(cum)).astype(_BF16)
        k_dec = (k * jnp.exp(-cum)).astype(_BF16)
        k_state = (k * jnp.exp(cum_last - cum)).astype(_BF16)
        a = lax.dot_general(q_dec, k_dec, _NT, preferred_element_type=_F32)
        a = jnp.where(tril, a, 0.0).astype(_BF16)
        state_t = state_ref[...]
        o = jnp.dot(a, v, preferred_element_type=_F32)
        o = o + lax.dot_general(q_dec, state_t.astype(_BF16), _NT,
                                preferred_element_type=_F32)
        delta_t = lax.dot_general(v, k_state, _TN, preferred_element_type=_F32)
        state_ref[...] = state_t * jnp.exp(cum_last) + delta_t
        o = o * lax.rsqrt(jnp.mean(o * o, axis=-1, keepdims=True) + RMS_EPS) * gain
        r = r_ref[rows, :]
        o_ref[rows, :] = (o * (r * jax.nn.sigmoid(r))).astype(o_ref.dtype)
        return carry

    lax.fori_loop(0, n_chunks, chunk, 0)


def _gla(proj, gate_up, gate_bias, norm_g, *, batch, seq, q_col, k_col, v_col, r_col, lr_col):
    dk, dv = GLA_KEY_DIM, GLA_VAL_DIM
    return pl.pallas_call(
        functools.partial(_gla_kernel, n_chunks=seq // GLA_CHUNK),
        grid=(batch, GLA_HEADS),
        in_specs=[pl.BlockSpec((seq, dk), lambda b, h: (b, q_col + h)),
                  pl.BlockSpec((seq, dk), lambda b, h: (b, k_col + h)),
                  pl.BlockSpec((seq, dv), lambda b, h: (b, v_col + h)),
                  pl.BlockSpec((seq, dv), lambda b, h: (b, r_col + h)),
                  pl.BlockSpec((seq, V7X_LANES), lambda b, h: (b, lr_col)),
                  pl.BlockSpec((V7X_LANES, dk), lambda b, h: (0, h)),
                  pl.BlockSpec((1, dk), lambda b, h: (0, h)),
                  pl.BlockSpec((1, dv), lambda b, h: (0, 0))],
        out_specs=pl.BlockSpec((seq, dv), lambda b, h: (b, h)),
        out_shape=jax.ShapeDtypeStruct((batch * seq, GLA_HEADS * dv), _BF16),
        scratch_shapes=[pltpu.VMEM((dv, dk), _F32)],
        compiler_params=_params("parallel", "parallel"),
        name="gla",
    )(proj, proj, proj, proj, proj, gate_up, gate_bias, norm_g)


def _xattn_kernel(q_ref, kv_ref, o_ref):
    d = q_ref.shape[1]
    hd = d // XATTN_HEADS
    scale = hd ** -0.5
    for h in range(XATTN_HEADS):
        qh = q_ref[:, h * hd:(h + 1) * hd]
        kh = kv_ref[:, h * hd:(h + 1) * hd]
        vh = kv_ref[:, d + h * hd:d + (h + 1) * hd]
        s = lax.dot_general(qh, kh, _NT, preferred_element_type=_F32) * scale
        m = jnp.max(s, axis=-1, keepdims=True)
        p = jnp.exp(s - m)
        l = jnp.sum(p, axis=-1, keepdims=True)
        o = jnp.dot(p.astype(_BF16), vh, preferred_element_type=_F32)
        o_ref[:, h * hd:(h + 1) * hd] = (o / l).astype(o_ref.dtype)


def _xattn(q, kv, *, batch, seq, mem_len, bq):
    d = q.shape[1]
    nq = seq // bq
    return pl.pallas_call(
        _xattn_kernel,
        grid=(batch, nq),
        in_specs=[pl.BlockSpec((bq, d), lambda b, i: (b * nq + i, 0)),
                  pl.BlockSpec((mem_len, 2 * d), lambda b, i: (b, 0))],
        out_specs=pl.BlockSpec((bq, d), lambda b, i: (b * nq + i, 0)),
        out_shape=jax.ShapeDtypeStruct((batch * seq, d), _BF16),
        compiler_params=_params("parallel", "parallel"),
        name="xattn",
    )(q, kv)


def _top_values(s, k):
    tm = s.shape[1]
    rid = lax.broadcasted_iota(jnp.int32, (k, tm), 0)
    top = jnp.zeros((k, tm), _F32)
    for r in range(k):
        m = jnp.max(s, axis=0, keepdims=True)
        top = jnp.where(rid == r, m, top)
        s = jnp.where(s == m, -jnp.inf, s)
    return top


def _peer_select_kernel(q_ref, keys_ref, s1_ref, s2_ref, e1_ref, e2_ref, tau_ref):
    kk = PEER_TOPK
    dq = keys_ref.shape[3]
    hi = lax.Precision.HIGHEST
    for h in range(PEER_HEADS):
        q1 = q_ref[:, (2 * h) * dq:(2 * h + 1) * dq]
        q2 = q_ref[:, (2 * h + 1) * dq:(2 * h + 2) * dq]
        s1 = lax.dot_general(keys_ref[h, 0], q1, _NT, precision=hi,
                             preferred_element_type=_F32)
        s2 = lax.dot_general(keys_ref[h, 1], q2, _NT, precision=hi,
                             preferred_element_type=_F32)
        a = _top_values(s1, kk)
        b = _top_values(s2, kk)
        cand = jnp.concatenate([a[r:r + 1, :] + b for r in range(kk)], axis=0)
        best = _top_values(cand, kk)
        tau = best[kk - 1:kk, :]
        a0 = a[0:1, :]
        b0 = b[0:1, :]
        z = jnp.sum(jnp.exp(best - (a0 + b0)), axis=0, keepdims=True)
        s1_ref[h] = s1
        s2_ref[h] = s2
        e1_ref[h] = jnp.exp(s1 - a0)
        e2_ref[h] = jnp.exp(s2 - b0) / z
        tau_ref[h:h + 1, :] = tau


def _peer_select(q, sub_keys, *, tm):
    t = q.shape[0]
    nk = PEER_N_KEYS
    big = jax.ShapeDtypeStruct((PEER_HEADS, nk, t), _F32)
    big_spec = pl.BlockSpec((PEER_HEADS, nk, tm), lambda i: (0, 0, i))
    return pl.pallas_call(
        _peer_select_kernel,
        grid=(t // tm,),
        in_specs=[pl.BlockSpec((tm, q.shape[1]), lambda i: (i, 0)),
                  pl.BlockSpec(sub_keys.shape, lambda i: (0, 0, 0, 0))],
        out_specs=[big_spec, big_spec, big_spec, big_spec,
                   pl.BlockSpec((PEER_HEADS, tm), lambda i: (0, i))],
        out_shape=[big, big, big, big, jax.ShapeDtypeStruct((PEER_HEADS, t), _F32)],
        compiler_params=_params("parallel"),
        name="peer_select",
    )(q, sub_keys)


PEER_ROWS_PER_CHUNK = 8


def _peer_experts_kernel(xt_ref, u_ref, vt_ref, s1_ref, e1_ref, s2_ref, e2_ref, tau_ref,
                         o_ref, g_ref):
    e = pl.program_id(1)
    nk = PEER_N_KEYS
    tm = xt_ref.shape[1]
    hdn = jnp.dot(u_ref[...], xt_ref[...], preferred_element_type=_F32)
    sqrt_half = math.sqrt(0.5)
    for lc in range(tm // V7X_LANES):
        lanes = slice(lc * V7X_LANES, (lc + 1) * V7X_LANES)
        for ii in range(PEER_ROWS_PER_CHUNK):
            w = jnp.zeros((nk, V7X_LANES), _F32)
            for h in range(PEER_HEADS):
                s1r = s1_ref[h, ii:ii + 1, lanes]
                e1r = e1_ref[h, ii:ii + 1, lanes]
                taur = tau_ref[h:h + 1, lanes]
                picked = (s1r + s2_ref[h, :, lanes]) >= taur
                w = w + jnp.where(picked, e2_ref[h, :, lanes], 0.0) * e1r
            hh = hdn[ii * nk:(ii + 1) * nk, lanes]
            act = 0.5 * hh * (1.0 + lax.erf(hh * sqrt_half))
            g_ref[ii * nk:(ii + 1) * nk, lanes] = (w * act).astype(_BF16)
    contrib = jnp.dot(vt_ref[...], g_ref[...], preferred_element_type=_F32)

    @pl.when(e == 0)
    def _():
        o_ref[...] = contrib

    @pl.when(e != 0)
    def _():
        o_ref[...] += contrib


def _peer_experts(xt, u, vt, s1, s2, e1, e2, tau, *, tm):
    d, t = xt.shape
    n_exp = u.shape[0]
    nk = PEER_N_KEYS
    rows = PEER_ROWS_PER_CHUNK
    chunk = rows * nk
    return pl.pallas_call(
        _peer_experts_kernel,
        grid=(t // tm, n_exp // chunk),
        in_specs=[pl.BlockSpec((d, tm), lambda i, e: (0, i)),
                  pl.BlockSpec((chunk, d), lambda i, e: (e, 0)),
                  pl.BlockSpec((d, chunk), lambda i, e: (0, e)),
                  pl.BlockSpec((PEER_HEADS, rows, tm), lambda i, e: (0, e, i)),
                  pl.BlockSpec((PEER_HEADS, rows, tm), lambda i, e: (0, e, i)),
                  pl.BlockSpec((PEER_HEADS, nk, tm), lambda i, e: (0, 0, i)),
                  pl.BlockSpec((PEER_HEADS, nk, tm), lambda i, e: (0, 0, i)),
                  pl.BlockSpec((PEER_HEADS, tm), lambda i, e: (0, i))],
        out_specs=pl.BlockSpec((d, tm), lambda i, e: (0, i)),
        out_shape=jax.ShapeDtypeStruct((d, t), _F32),
        scratch_shapes=[pltpu.VMEM((chunk, tm), _BF16)],
        compiler_params=_params("parallel", "arbitrary"),
        name="peer_experts",
    )(xt, u, vt, s1, e1, s2, e2, tau)


def _residual_ln_t_kernel(ft_ref, r_ref, g_ref, b_ref, o_ref):
    ff = ft_ref[...].T
    o_ref[...] = _layer_norm_rows(DEEPNORM_ALPHA * r_ref[...] + ff, g_ref[...], b_ref[...])


def _residual_ln_t(ff_t, resid, g, b, *, bm):
    m, d = resid.shape
    return pl.pallas_call(
        _residual_ln_t_kernel,
        grid=(m // bm,),
        in_specs=[pl.BlockSpec((d, bm), lambda i: (0, i)),
                  pl.BlockSpec((bm, d), lambda i: (i, 0)),
                  pl.BlockSpec((1, d), lambda i: (0, 0)),
                  pl.BlockSpec((1, d), lambda i: (0, 0))],
        out_specs=pl.BlockSpec((bm, d), lambda i: (i, 0)),
        out_shape=jax.ShapeDtypeStruct((m, d), _F32),
        compiler_params=_params("parallel"),
        name="residual_ln_t",
    )(ff_t, resid, g, b)


def _layer(x, mem, w_in, gate_up, gate_bias, norm_g, w_out, ln1_g, ln1_b,
           wq, wkv, wo, ln2_g, ln2_b, peer_wq, sub_keys, peer_u, peer_v, ln3_g, ln3_b):
    batch, seq, d = x.shape
    mem_len = mem.shape[1]
    t = batch * seq
    x2 = x.reshape(t, d)

    moba_w = MOBA_HEADS * MOBA_HEAD_DIM
    gla_kw = GLA_HEADS * GLA_KEY_DIM
    gla_vw = GLA_HEADS * GLA_VAL_DIM
    cols = w_in.shape[1]
    cols_pad = -(-cols // V7X_LANES) * V7X_LANES
    w_in_p = jnp.pad(w_in, ((0, 0), (0, cols_pad - cols))).astype(_BF16)
    proj = _matmul(x2.astype(_BF16), w_in_p, bm=1024, bn=cols_pad // 7, out_dtype=_F32)

    q_m, k_m, v_m = 0, moba_w, 2 * moba_w
    q_g = 3 * moba_w
    k_g = q_g + gla_kw
    v_g = k_g + gla_kw
    r_g = v_g + gla_vw
    lr = r_g + gla_vw
    o_moba = _moba(proj, batch=batch, seq=seq, q_col=q_m // MOBA_HEAD_DIM,
                   k_col=k_m // MOBA_HEAD_DIM, v_col=v_m // MOBA_HEAD_DIM)
    gate_up_p = jnp.pad(gate_up, ((0, V7X_LANES - gate_up.shape[0]), (0, 0)))
    o_gla = _gla(proj, gate_up_p, gate_bias.reshape(1, -1), norm_g.reshape(1, -1),
                 batch=batch, seq=seq, q_col=q_g // GLA_KEY_DIM, k_col=k_g // GLA_KEY_DIM,
                 v_col=v_g // GLA_VAL_DIM, r_col=r_g // GLA_VAL_DIM, lr_col=lr // V7X_LANES)
    h1, h1_bf = _matmul_residual_ln(o_moba, 0, o_gla, 0, w_out.astype(_BF16), x2,
                                    ln1_g.reshape(1, -1), ln1_b.reshape(1, -1), bm=512)

    q_x = _matmul(h1_bf, wq.astype(_BF16), bm=1024, bn=1024, out_dtype=_BF16)
    kv = _matmul(mem.reshape(batch * mem_len, d).astype(_BF16), wkv.astype(_BF16),
                 bm=1024, bn=1024, out_dtype=_BF16)
    o_x = _xattn(q_x, kv, batch=batch, seq=seq, mem_len=mem_len, bq=512)
    h2, h2_bf = _matmul_residual_ln(o_x, 0, o_x, 1, wo.astype(_BF16), h1,
                                    ln2_g.reshape(1, -1), ln2_b.reshape(1, -1), bm=512)

    q_p = _matmul(h2_bf, peer_wq.astype(_BF16), bm=1024, bn=1024, out_dtype=_F32)
    s1, s2, e1, e2, tau = _peer_select(q_p, sub_keys, tm=256)
    ff_t = _peer_experts(h2_bf.T, peer_u.astype(_BF16), peer_v.astype(_BF16).T,
                         s1, s2, e1, e2, tau, tm=512)
    y = _residual_ln_t(ff_t, h2, ln3_g.reshape(1, -1), ln3_b.reshape(1, -1), bm=512)
    return y.reshape(batch, seq, d)


def kernel(x, mem, w_in, gla_gate_up, gla_gate_bias, gla_norm_g, w_out, ln1_g, ln1_b,
           xattn_wq, xattn_wkv, xattn_wo, ln2_g, ln2_b,
           peer_wq, peer_sub_keys, peer_u, peer_v, ln3_g, ln3_b):
    h = x
    for l in range(w_in.shape[0]):
        h = _layer(h, mem, w_in[l], gla_gate_up[l], gla_gate_bias[l], gla_norm_g[l], w_out[l],
                   ln1_g[l], ln1_b[l], xattn_wq[l], xattn_wkv[l], xattn_wo[l], ln2_g[l], ln2_b[l],
                   peer_wq[l], peer_sub_keys[l], peer_u[l], peer_v[l], ln3_g[l], ln3_b[l])
    return h
```

```python
import functools
import math

import jax
import jax.numpy as jnp
import numpy as np
from jax import lax
from jax.experimental import pallas as pl
from jax.experimental.pallas import tpu as pltpu

MOBA_HEADS = 8
MOBA_HEAD_DIM = 128
MOBA_BLOCK = 256
MOBA_TOPK = 3
GLA_HEADS = 4
GLA_KEY_DIM = 128
GLA_VAL_DIM = 256
GLA_GATE_RANK = 16
GLA_GATE_TAU = 16.0
GLA_CHUNK = 64
XATTN_HEADS = 4
PEER_HEADS = 8
PEER_N_KEYS = 128
PEER_TOPK = 16
DEPTH = 1
DEEPNORM_ALPHA = (2.0 * DEPTH) ** 0.25
LN_EPS = 1e-5
RMS_EPS = 1e-6
NEG_INF = -1e30

V7X_LANES = 128
V7X_SUBLANES = 8
V7X_VMEM_LIMIT_BYTES = 56 * 1024 * 1024

_BF16 = jnp.bfloat16
_F32 = jnp.float32
_NT = (((1,), (1,)), ((), ()))
_TN = (((0,), (0,)), ((), ()))


def _params(*semantics):
    return pltpu.CompilerParams(dimension_semantics=semantics,
                                vmem_limit_bytes=V7X_VMEM_LIMIT_BYTES)


def _matmul_kernel(a_ref, b_ref, o_ref):
    o_ref[...] = jnp.dot(a_ref[...], b_ref[...],
                         preferred_element_type=_F32).astype(o_ref.dtype)


def _matmul(a, b, *, bm, bn, out_dtype):
    m, k = a.shape
    _, n = b.shape
    bm = min(bm, m)
    assert m % bm == 0 and n % bn == 0
    return pl.pallas_call(
        _matmul_kernel,
        grid=(m // bm, n // bn),
        in_specs=[pl.BlockSpec((bm, k), lambda i, j: (i, 0)),
                  pl.BlockSpec((k, bn), lambda i, j: (0, j))],
        out_specs=pl.BlockSpec((bm, bn), lambda i, j: (i, j)),
        out_shape=jax.ShapeDtypeStruct((m, n), out_dtype),
        compiler_params=_params("parallel", "parallel"),
        name="matmul",
    )(a, b)


def _layer_norm_rows(z, g, b):
    mu = jnp.mean(z, axis=-1, keepdims=True)
    zc = z - mu
    var = jnp.mean(zc * zc, axis=-1, keepdims=True)
    return zc * lax.rsqrt(var + LN_EPS) * g + b


def _matmul_residual_ln_kernel(a1_ref, a2_ref, w_ref, r_ref, g_ref, b_ref, o_ref, obf_ref):
    kh = a1_ref.shape[1]
    acc = jnp.dot(a1_ref[...], w_ref[:kh, :], preferred_element_type=_F32)
    acc = acc + jnp.dot(a2_ref[...], w_ref[kh:, :], preferred_element_type=_F32)
    y = _layer_norm_rows(DEEPNORM_ALPHA * r_ref[...] + acc, g_ref[...], b_ref[...])
    o_ref[...] = y
    obf_ref[...] = y.astype(_BF16)


def _matmul_residual_ln(a1, a1_col, a2, a2_col, w, resid, g, b, *, bm):
    m, d = resid.shape
    k = w.shape[0]
    kh = k // 2
    return pl.pallas_call(
        _matmul_residual_ln_kernel,
        grid=(m // bm,),
        in_specs=[pl.BlockSpec((bm, kh), lambda i: (i, a1_col)),
                  pl.BlockSpec((bm, kh), lambda i: (i, a2_col)),
                  pl.BlockSpec((k, d), lambda i: (0, 0)),
                  pl.BlockSpec((bm, d), lambda i: (i, 0)),
                  pl.BlockSpec((1, d), lambda i: (0, 0)),
                  pl.BlockSpec((1, d), lambda i: (0, 0))],
        out_specs=[pl.BlockSpec((bm, d), lambda i: (i, 0)),
                   pl.BlockSpec((bm, d), lambda i: (i, 0))],
        out_shape=[jax.ShapeDtypeStruct((m, d), _F32),
                   jax.ShapeDtypeStruct((m, d), _BF16)],
        compiler_params=_params("parallel"),
        name="matmul_residual_ln",
    )(a1, a2, w, resid, g, b)


def _moba_select_blocks(q_ref, k_ref, sel_ref, n_blocks):
    bs = MOBA_BLOCK
    seq, dh = q_ref.shape
    kmean = jnp.sum(k_ref[...].reshape(n_blocks, bs, dh), axis=1) * (1.0 / bs)
    gate = lax.dot_general(kmean, q_ref[...], _NT, precision=lax.Precision.HIGHEST,
                           preferred_element_type=_F32)
    blk = lax.broadcasted_iota(jnp.int32, gate.shape, 0)
    tok = lax.broadcasted_iota(jnp.int32, gate.shape, 1)
    gate = jnp.where(tok >= (blk + 1) * bs, gate, NEG_INF)
    rank = jnp.zeros(gate.shape, _F32)
    for j in range(n_blocks):
        gj = gate[j:j + 1, :]
        ahead = jnp.logical_or(gj > gate, jnp.logical_and(gj == gate, blk > j))
        rank = rank + jnp.where(ahead, 1.0, 0.0)
    picked = jnp.logical_and(rank < MOBA_TOPK, gate > 0.5 * NEG_INF)
    sel_t = jnp.where(picked, 1.0, 0.0)
    sel_t = jnp.concatenate([sel_t, jnp.zeros((V7X_LANES - n_blocks, seq), _F32)], axis=0)
    sel_ref[...] = sel_t.T


def _moba_block_body(q_ref, k_ref, v_ref, o_ref, sel_ref, n_past):
    bs = MOBA_BLOCK
    dh = q_ref.shape[1]
    n_keys = (n_past + 1) * bs
    q = q_ref[n_past * bs:n_keys, :]
    qb = (q * (dh ** -0.5)).astype(_BF16)
    kb = k_ref[0:n_keys, :].astype(_BF16)
    vb = v_ref[0:n_keys, :].astype(_BF16)
    row = lax.broadcasted_iota(jnp.int32, (bs, bs), 0)
    col = lax.broadcasted_iota(jnp.int32, (bs, bs), 1)
    parts = []
    if n_past > 0:
        sel = sel_ref[n_past * bs:n_keys, :]
        for j in range(n_past):
            s = lax.dot_general(qb, kb[j * bs:(j + 1) * bs, :], _NT,
                                preferred_element_type=_F32)
            parts.append(jnp.where(sel[:, j:j + 1] > 0.0, s, NEG_INF))
    s = lax.dot_general(qb, kb[n_past * bs:n_keys, :], _NT, preferred_element_type=_F32)
    parts.append(jnp.where(col <= row, s, NEG_INF))
    s_all = jnp.concatenate(parts, axis=-1)
    m = jnp.max(s_all, axis=-1, keepdims=True)
    p = jnp.exp(s_all - m)
    l = jnp.sum(p, axis=-1, keepdims=True)
    o = jnp.dot(p.astype(_BF16), vb, preferred_element_type=_F32)
    o_ref[...] = (o / l).astype(o_ref.dtype)


def _moba_kernel(q_ref, k_ref, v_ref, o_ref, sel_ref, *, n_blocks):
    qi = pl.program_id(2)
    if n_blocks > 1:
        pl.when(qi == 0)(functools.partial(_moba_select_blocks, q_ref, k_ref, sel_ref, n_blocks))
    for c in range(n_blocks):
        pl.when(qi == c)(
            functools.partial(_moba_block_body, q_ref, k_ref, v_ref, o_ref, sel_ref, c))


def _moba(proj, *, batch, seq, q_col, k_col, v_col):
    nb = seq // MOBA_BLOCK
    dh = MOBA_HEAD_DIM
    return pl.pallas_call(
        functools.partial(_moba_kernel, n_blocks=nb),
        grid=(batch, MOBA_HEADS, nb),
        in_specs=[pl.BlockSpec((seq, dh), lambda b, h, i: (b, q_col + h)),
                  pl.BlockSpec((seq, dh), lambda b, h, i: (b, k_col + h)),
                  pl.BlockSpec((seq, dh), lambda b, h, i: (b, v_col + h))],
        out_specs=pl.BlockSpec((MOBA_BLOCK, dh), lambda b, h, i: (b * nb + i, h)),
        out_shape=jax.ShapeDtypeStruct((batch * seq, MOBA_HEADS * dh), _BF16),
        scratch_shapes=[pltpu.VMEM((seq, V7X_LANES), _F32)],
        compiler_params=_params("parallel", "parallel", "arbitrary"),
        name="moba",
    )(proj, proj, proj)


GLA_UNROLL = 4


def _log_sigmoid(z):
    return jnp.minimum(z, 0.0) - jnp.log(1.0 + jnp.exp(-jnp.abs(z)))


def _gla_kernel(q_ref, k_ref, v_ref, r_ref, lr_ref, up_ref, bias_ref, g_ref, o_ref,
                la_ref, qd_ref, ks_ref, oi_ref, dec_ref, *, n_chunks):
    c = GLA_CHUNK
    dk = q_ref.shape[1]
    dv = v_ref.shape[1]
    sub = V7X_SUBLANES
    row = lax.broadcasted_iota(jnp.int32, (c, c), 0)
    col = lax.broadcasted_iota(jnp.int32, (c, c), 1)
    tril = col <= row
    tril_f = tril.astype(_F32)
    gain = g_ref[...]
    hi = lax.Precision.HIGHEST

    z = jnp.dot(lr_ref[...], up_ref[...], precision=hi, preferred_element_type=_F32)
    la_ref[...] = _log_sigmoid(z + bias_ref[...]) * (1.0 / GLA_GATE_TAU)

    def intra(n, carry):
        rows = pl.ds(pl.multiple_of(n * c, c), c)
        cum = jnp.dot(tril_f, la_ref[rows, :], precision=hi, preferred_element_type=_F32)
        cum_last = cum[c - 1:c, :]
        q = q_ref[rows, :] * (dk ** -0.5)
        k = k_ref[rows, :]
        q_dec = (q * jnp.exp(cum)).astype(_BF16)
        k_dec = (k * jnp.exp(-cum)).astype(_BF16)
        a = lax.dot_general(q_dec, k_dec, _NT, preferred_element_type=_F32)
        a = jnp.where(tril, a, 0.0).astype(_BF16)
        qd_ref[rows, :] = q_dec
        ks_ref[rows, :] = (k * jnp.exp(cum_last - cum)).astype(_BF16)
        oi_ref[rows, :] = jnp.dot(a, v_ref[rows, :].astype(_BF16), preferred_element_type=_F32)
        dec_ref[pl.ds(pl.multiple_of(n * sub, sub), sub), :] = jnp.broadcast_to(
            jnp.exp(cum_last), (sub, dk))
        return carry

    lax.fori_loop(0, n_chunks, intra, 0, unroll=GLA_UNROLL)

    def inter(n, state_t):
        rows = pl.ds(pl.multiple_of(n * c, c), c)
        o = oi_ref[rows, :] + lax.dot_general(qd_ref[rows, :], state_t.astype(_BF16), _NT,
                                              preferred_element_type=_F32)
        delta_t = lax.dot_general(v_ref[rows, :].astype(_BF16), ks_ref[rows, :], _TN,
                                  preferred_element_type=_F32)
        decay = dec_ref[pl.ds(pl.multiple_of(n * sub, sub), 1), :]
        o = o * lax.rsqrt(jnp.mean(o * o, axis=-1, keepdims=True) + RMS_EPS) * gain
        r = r_ref[rows, :]
        o_ref[rows, :] = (o * (r * jax.nn.sigmoid(r))).astype(o_ref.dtype)
        return state_t * decay + delta_t

    lax.fori_loop(0, n_chunks, inter, jnp.zeros((dv, dk), _F32), unroll=GLA_UNROLL)


def _gla(proj, gate_up, gate_bias, norm_g, *, batch, seq, q_col, k_col, v_col, r_col, lr_col):
    dk, dv = GLA_KEY_DIM, GLA_VAL_DIM
    return pl.pallas_call(
        functools.partial(_gla_kernel, n_chunks=seq // GLA_CHUNK),
        grid=(batch, GLA_HEADS),
        in_specs=[pl.BlockSpec((seq, dk), lambda b, h: (b, q_col + h)),
                  pl.BlockSpec((seq, dk), lambda b, h: (b, k_col + h)),
                  pl.BlockSpec((seq, dv), lambda b, h: (b, v_col + h)),
                  pl.BlockSpec((seq, dv), lambda b, h: (b, r_col + h)),
                  pl.BlockSpec((seq, V7X_LANES), lambda b, h: (b, lr_col)),
                  pl.BlockSpec((V7X_LANES, dk), lambda b, h: (0, h)),
                  pl.BlockSpec((1, dk), lambda b, h: (0, h)),
                  pl.BlockSpec((1, dv), lambda b, h: (0, 0))],
        out_specs=pl.BlockSpec((seq, dv), lambda b, h: (b, h)),
        out_shape=jax.ShapeDtypeStruct((batch * seq, GLA_HEADS * dv), _BF16),
        scratch_shapes=[pltpu.VMEM((seq, dk), _F32),
                        pltpu.VMEM((seq, dk), _BF16),
                        pltpu.VMEM((seq, dk), _BF16),
                        pltpu.VMEM((seq, dv), _F32),
                        pltpu.VMEM((seq // GLA_CHUNK * V7X_SUBLANES, dk), _F32)],
        compiler_params=_params("parallel", "parallel"),
        name="gla",
    )(proj, proj, proj, proj, proj, gate_up, gate_bias, norm_g)


def _xattn_kernel(q_ref, kv_ref, o_ref):
    d = q_ref.shape[1]
    hd = d // XATTN_HEADS
    scale = hd ** -0.5
    for h in range(XATTN_HEADS):
        qh = q_ref[:, h * hd:(h + 1) * hd]
        kh = kv_ref[:, h * hd:(h + 1) * hd]
        vh = kv_ref[:, d + h * hd:d + (h + 1) * hd]
        s = lax.dot_general(qh, kh, _NT, preferred_element_type=_F32) * scale
        m = jnp.max(s, axis=-1, keepdims=True)
        p = jnp.exp(s - m)
        l = jnp.sum(p, axis=-1, keepdims=True)
        o = jnp.dot(p.astype(_BF16), vh, preferred_element_type=_F32)
        o_ref[:, h * hd:(h + 1) * hd] = (o / l).astype(o_ref.dtype)


def _xattn(q, kv, *, batch, seq, mem_len, bq):
    d = q.shape[1]
    nq = seq // bq
    return pl.pallas_call(
        _xattn_kernel,
        grid=(batch, nq),
        in_specs=[pl.BlockSpec((bq, d), lambda b, i: (b * nq + i, 0)),
                  pl.BlockSpec((mem_len, 2 * d), lambda b, i: (b, 0))],
        out_specs=pl.BlockSpec((bq, d), lambda b, i: (b * nq + i, 0)),
        out_shape=jax.ShapeDtypeStruct((batch * seq, d), _BF16),
        compiler_params=_params("parallel", "parallel"),
        name="xattn",
    )(q, kv)


def _top_values(s, k, with_rank=False):
    tm = s.shape[1]
    rid = lax.broadcasted_iota(jnp.int32, (k, tm), 0)
    top = jnp.zeros((k, tm), _F32)
    rank = jnp.full(s.shape, float(k), _F32)
    for r in range(k):
        m = jnp.max(s, axis=0, keepdims=True)
        top = jnp.where(rid == r, m, top)
        hit = s == m
        if with_rank:
            rank = jnp.where(hit, float(r), rank)
        s = jnp.where(hit, -jnp.inf, s)
    return (top, rank) if with_rank else top


def _pair_bf16(x):
    u = pltpu.bitcast(x.astype(_BF16).astype(_F32), jnp.uint32)
    return u | lax.shift_right_logical(u, jnp.uint32(16))


def _peer_select_kernel(q_ref, keys_ref, cnt_ref, e1_ref, rank_ref, e2_ref):
    kk = PEER_TOPK
    dq = keys_ref.shape[3]
    hi = lax.Precision.HIGHEST
    for h in range(PEER_HEADS):
        q1 = q_ref[:, (2 * h) * dq:(2 * h + 1) * dq]
        q2 = q_ref[:, (2 * h + 1) * dq:(2 * h + 2) * dq]
        s1 = lax.dot_general(keys_ref[h, 0], q1, _NT, precision=hi,
                             preferred_element_type=_F32)
        s2 = lax.dot_general(keys_ref[h, 1], q2, _NT, precision=hi,
                             preferred_element_type=_F32)
        a = _top_values(s1, kk)
        b, rank2 = _top_values(s2, kk, with_rank=True)
        half = kk // 2
        groups = [a[0:1, :] + b[0:half, :], a[0:1, :] + b[half:kk, :]]
        groups += [a[r:r + 1, :] + b[0:half, :] for r in range(1, half)]
        groups += [a[half:kk, :] + b[0:1, :]]
        best = _top_values(jnp.concatenate(groups, axis=0), kk)
        tau = best[kk - 1:kk, :]
        a0 = a[0:1, :]
        b0 = b[0:1, :]
        z = jnp.sum(jnp.exp(best - (a0 + b0)), axis=0, keepdims=True)
        cnt1 = jnp.zeros(s1.shape, _F32)
        for c in range(kk):
            cnt1 = cnt1 + jnp.where(s1 + b[c:c + 1, :] >= tau, 1.0, 0.0)
        cnt_ref[h] = _pair_bf16(cnt1)
        e1_ref[h] = _pair_bf16(jnp.exp(s1 - a0))
        rank_ref[h] = rank2.astype(_BF16)
        e2_ref[h] = (jnp.exp(s2 - b0) / z).astype(_BF16)


def _peer_select(q, sub_keys, *, tm):
    t = q.shape[0]
    nk = PEER_N_KEYS
    spec = pl.BlockSpec((PEER_HEADS, nk, tm), lambda i: (0, 0, i))
    pair = jax.ShapeDtypeStruct((PEER_HEADS, nk, t), jnp.uint32)
    half = jax.ShapeDtypeStruct((PEER_HEADS, nk, t), _BF16)
    return pl.pallas_call(
        _peer_select_kernel,
        grid=(t // tm,),
        in_specs=[pl.BlockSpec((tm, q.shape[1]), lambda i: (i, 0)),
                  pl.BlockSpec(sub_keys.shape, lambda i: (0, 0, 0, 0))],
        out_specs=[spec, spec, spec, spec],
        out_shape=[pair, pair, half, half],
        compiler_params=_params("parallel"),
        name="peer_select",
    )(q, sub_keys)


PEER_ROWS_PER_CHUNK = V7X_SUBLANES
PEER_LANE_CHUNK = 2 * V7X_LANES


def _bf16_rows(pair_row):
    tile = jnp.broadcast_to(pair_row, (V7X_SUBLANES, pair_row.shape[1]))
    return pltpu.bitcast(tile, _BF16)[0:V7X_SUBLANES]


def _peer_experts_kernel(xt_ref, u_ref, vt_ref, cnt_ref, e1_ref, rank_ref, e2_ref,
                         o_ref, hdn_ref, g_ref):
    e = pl.program_id(1)
    nk = PEER_N_KEYS
    tm = xt_ref.shape[1]
    sl = V7X_SUBLANES
    lw = PEER_LANE_CHUNK

    @pl.when(e == 0)
    def _():
        o_ref[...] = jnp.zeros_like(o_ref)

    hdn_ref[...] = jnp.dot(u_ref[...], xt_ref[...], preferred_element_type=_F32)
    sqrt_half = math.sqrt(0.5)
    for lc in range(tm // lw):
        lanes = slice(lc * lw, (lc + 1) * lw)
        for ii in range(PEER_ROWS_PER_CHUNK):
            w = None
            for h in range(PEER_HEADS):
                cnt = _bf16_rows(cnt_ref[h, ii:ii + 1, lanes])[None]
                e1 = _bf16_rows(e1_ref[h, ii:ii + 1, lanes])[None]
                rank = rank_ref[h, :, lanes].reshape(nk // sl, sl, lw)
                e2 = e2_ref[h, :, lanes].reshape(nk // sl, sl, lw)
                term = jnp.where(rank < cnt, e2, jnp.zeros_like(e2)) * e1
                w = term if w is None else w + term
            hh = hdn_ref[ii * nk:(ii + 1) * nk, lanes]
            act = (0.5 * hh) * (1.0 + lax.erf(hh * sqrt_half))
            g_ref[ii * nk:(ii + 1) * nk, lanes] = w.reshape(nk, lw) * act.astype(_BF16)
    o_ref[...] += jnp.dot(vt_ref[...], g_ref[...], preferred_element_type=_F32)


def _peer_experts(xt, u, vt, cnt, e1, rank, e2, *, tm):
    d, t = xt.shape
    n_exp = u.shape[0]
    nk = PEER_N_KEYS
    rows = PEER_ROWS_PER_CHUNK
    chunk = rows * nk
    return pl.pallas_call(
        _peer_experts_kernel,
        grid=(t // tm, n_exp // chunk),
        in_specs=[pl.BlockSpec((d, tm), lambda i, e: (0, i)),
                  pl.BlockSpec((chunk, d), lambda i, e: (e, 0)),
                  pl.BlockSpec((d, chunk), lambda i, e: (0, e)),
                  pl.BlockSpec((PEER_HEADS, rows, tm), lambda i, e: (0, e, i)),
                  pl.BlockSpec((PEER_HEADS, rows, tm), lambda i, e: (0, e, i)),
                  pl.BlockSpec((PEER_HEADS, nk, tm), lambda i, e: (0, 0, i)),
                  pl.BlockSpec((PEER_HEADS, nk, tm), lambda i, e: (0, 0, i))],
        out_specs=pl.BlockSpec((d, tm), lambda i, e: (0, i)),
        out_shape=jax.ShapeDtypeStruct((d, t), _F32),
        scratch_shapes=[pltpu.VMEM((chunk, tm), _F32), pltpu.VMEM((chunk, tm), _BF16)],
        compiler_params=_params("parallel", "arbitrary"),
        name="peer_experts",
    )(xt, u, vt, cnt, e1, rank, e2)


def _residual_ln_t_kernel(ft_ref, r_ref, g_ref, b_ref, o_ref):
    ff = ft_ref[...].T
    o_ref[...] = _layer_norm_rows(DEEPNORM_ALPHA * r_ref[...] + ff, g_ref[...], b_ref[...])


def _residual_ln_t(ff_t, resid, g, b, *, bm):
    m, d = resid.shape
    return pl.pallas_call(
        _residual_ln_t_kernel,
        grid=(m // bm,),
        in_specs=[pl.BlockSpec((d, bm), lambda i: (0, i)),
                  pl.BlockSpec((bm, d), lambda i: (i, 0)),
                  pl.BlockSpec((1, d), lambda i: (0, 0)),
                  pl.BlockSpec((1, d), lambda i: (0, 0))],
        out_specs=pl.BlockSpec((bm, d), lambda i: (i, 0)),
        out_shape=jax.ShapeDtypeStruct((m, d), _F32),
        compiler_params=_params("parallel"),
        name="residual_ln_t",
    )(ff_t, resid, g, b)


def _layer(x, mem, w_in, gate_up, gate_bias, norm_g, w_out, ln1_g, ln1_b,
           wq, wkv, wo, ln2_g, ln2_b, peer_wq, sub_keys, peer_u, peer_v, ln3_g, ln3_b):
    batch, seq, d = x.shape
    mem_len = mem.shape[1]
    t = batch * seq
    x2 = x.reshape(t, d)

    moba_w = MOBA_HEADS * MOBA_HEAD_DIM
    gla_kw = GLA_HEADS * GLA_KEY_DIM
    gla_vw = GLA_HEADS * GLA_VAL_DIM
    cols = w_in.shape[1]
    cols_pad = -(-cols // V7X_LANES) * V7X_LANES
    w_in_p = jnp.pad(w_in, ((0, 0), (0, cols_pad - cols))).astype(_BF16)
    proj = _matmul(x2.astype(_BF16), w_in_p, bm=1024, bn=cols_pad // 7, out_dtype=_F32)

    q_m, k_m, v_m = 0, moba_w, 2 * moba_w
    q_g = 3 * moba_w
    k_g = q_g + gla_kw
    v_g = k_g + gla_kw
    r_g = v_g + gla_vw
    lr = r_g + gla_vw
    o_moba = _moba(proj, batch=batch, seq=seq, q_col=q_m // MOBA_HEAD_DIM,
                   k_col=k_m // MOBA_HEAD_DIM, v_col=v_m // MOBA_HEAD_DIM)
    gate_up_p = jnp.pad(gate_up, ((0, V7X_LANES - gate_up.shape[0]), (0, 0)))
    o_gla = _gla(proj, gate_up_p, gate_bias.reshape(1, -1), norm_g.reshape(1, -1),
                 batch=batch, seq=seq, q_col=q_g // GLA_KEY_DIM, k_col=k_g // GLA_KEY_DIM,
                 v_col=v_g // GLA_VAL_DIM, r_col=r_g // GLA_VAL_DIM, lr_col=lr // V7X_LANES)
    h1, h1_bf = _matmul_residual_ln(o_moba, 0, o_gla, 0, w_out.astype(_BF16), x2,
                                    ln1_g.reshape(1, -1), ln1_b.reshape(1, -1), bm=512)

    q_x = _matmul(h1_bf, wq.astype(_BF16), bm=1024, bn=1024, out_dtype=_BF16)
    kv = _matmul(mem.reshape(batch * mem_len, d).astype(_BF16), wkv.astype(_BF16),
                 bm=1024, bn=1024, out_dtype=_BF16)
    o_x = _xattn(q_x, kv, batch=batch, seq=seq, mem_len=mem_len, bq=512)
    h2, h2_bf = _matmul_residual_ln(o_x, 0, o_x, 1, wo.astype(_BF16), h1,
                                    ln2_g.reshape(1, -1), ln2_b.reshape(1, -1), bm=512)

    q_p = _matmul(h2_bf, peer_wq.astype(_BF16), bm=1024, bn=1024, out_dtype=_F32)
    cnt, e1, rank, e2 = _peer_select(q_p, sub_keys, tm=256)
    ff_t = _peer_experts(h2_bf.T, peer_u.astype(_BF16), peer_v.astype(_BF16).T,
                         cnt, e1, rank, e2, tm=512)
    y = _residual_ln_t(ff_t, h2, ln3_g.reshape(1, -1), ln3_b.reshape(1, -1), bm=512)
    return y.reshape(batch, seq, d)


def kernel(x, mem, w_in, gla_gate_up, gla_gate_bias, gla_norm_g, w_out, ln1_g, ln1_b,
           xattn_wq, xattn_wkv, xattn_wo, ln2_g, ln2_b,
           peer_wq, peer_sub_keys, peer_u, peer_v, ln3_g, ln3_b):
    h = x
    for l in range(w_in.shape[0]):
        h = _layer(h, mem, w_in[l], gla_gate_up[l], gla_gate_bias[l], gla_norm_g[l], w_out[l],
                   ln1_g[l], ln1_b[l], xattn_wq[l], xattn_wkv[l], xattn_wo[l], ln2_g[l], ln2_b[l],
                   peer_wq[l], peer_sub_keys[l], peer_u[l], peer_v[l], ln3_g[l], ln3_b[l])
    return h
```

```python
import functools
import math

import jax
import jax.numpy as jnp
import numpy as np
from jax import lax
from jax.experimental import pallas as pl
from jax.experimental.pallas import tpu as pltpu

MOBA_HEADS = 8
MOBA_HEAD_DIM = 128
MOBA_BLOCK = 256
MOBA_TOPK = 3
GLA_HEADS = 4
GLA_KEY_DIM = 128
GLA_VAL_DIM = 256
GLA_GATE_RANK = 16
GLA_GATE_TAU = 16.0
GLA_CHUNK = 64
XATTN_HEADS = 4
PEER_HEADS = 8
PEER_N_KEYS = 128
PEER_TOPK = 16
DEPTH = 1
DEEPNORM_ALPHA = (2.0 * DEPTH) ** 0.25
LN_EPS = 1e-5
RMS_EPS = 1e-6
NEG_INF = -1e30

V7X_LANES = 128
V7X_SUBLANES = 8
V7X_VMEM_LIMIT_BYTES = 56 * 1024 * 1024

_BF16 = jnp.bfloat16
_F32 = jnp.float32
_NT = (((1,), (1,)), ((), ()))
_TN = (((0,), (0,)), ((), ()))


def _params(*semantics):
    return pltpu.CompilerParams(dimension_semantics=semantics,
                                vmem_limit_bytes=V7X_VMEM_LIMIT_BYTES)


def _matmul_kernel(a_ref, b_ref, o_ref):
    o_ref[...] = jnp.dot(a_ref[...], b_ref[...],
                         preferred_element_type=_F32).astype(o_ref.dtype)


def _matmul(a, b, *, bm, bn, out_dtype):
    m, k = a.shape
    _, n = b.shape
    bm = min(bm, m)
    assert m % bm == 0 and n % bn == 0
    return pl.pallas_call(
        _matmul_kernel,
        grid=(m // bm, n // bn),
        in_specs=[pl.BlockSpec((bm, k), lambda i, j: (i, 0)),
                  pl.BlockSpec((k, bn), lambda i, j: (0, j))],
        out_specs=pl.BlockSpec((bm, bn), lambda i, j: (i, j)),
        out_shape=jax.ShapeDtypeStruct((m, n), out_dtype),
        compiler_params=_params("parallel", "parallel"),
        name="matmul",
    )(a, b)


def _layer_norm_rows(z, g, b):
    mu = jnp.mean(z, axis=-1, keepdims=True)
    zc = z - mu
    var = jnp.mean(zc * zc, axis=-1, keepdims=True)
    return zc * lax.rsqrt(var + LN_EPS) * g + b


def _matmul_residual_ln_kernel(a1_ref, a2_ref, w_ref, r_ref, g_ref, b_ref, o_ref, obf_ref,
                               *maybe_obt_ref):
    kh = a1_ref.shape[1]
    acc = jnp.dot(a1_ref[...], w_ref[:kh, :], preferred_element_type=_F32)
    acc = acc + jnp.dot(a2_ref[...], w_ref[kh:, :], preferred_element_type=_F32)
    y = _layer_norm_rows(DEEPNORM_ALPHA * r_ref[...] + acc, g_ref[...], b_ref[...])
    o_ref[...] = y
    obf_ref[...] = y.astype(_BF16)
    for obt_ref in maybe_obt_ref:
        obt_ref[...] = y.T.astype(_BF16)


def _matmul_residual_ln(a1, a1_col, a2, a2_col, w, resid, g, b, *, bm, transposed_copy=False):
    m, d = resid.shape
    k = w.shape[0]
    kh = k // 2
    out_specs = [pl.BlockSpec((bm, d), lambda i: (i, 0)), pl.BlockSpec((bm, d), lambda i: (i, 0))]
    out_shape = [jax.ShapeDtypeStruct((m, d), _F32), jax.ShapeDtypeStruct((m, d), _BF16)]
    if transposed_copy:
        out_specs.append(pl.BlockSpec((d, bm), lambda i: (0, i)))
        out_shape.append(jax.ShapeDtypeStruct((d, m), _BF16))
    return pl.pallas_call(
        _matmul_residual_ln_kernel,
        grid=(m // bm,),
        in_specs=[pl.BlockSpec((bm, kh), lambda i: (i, a1_col)),
                  pl.BlockSpec((bm, kh), lambda i: (i, a2_col)),
                  pl.BlockSpec((k, d), lambda i: (0, 0)),
                  pl.BlockSpec((bm, d), lambda i: (i, 0)),
                  pl.BlockSpec((1, d), lambda i: (0, 0)),
                  pl.BlockSpec((1, d), lambda i: (0, 0))],
        out_specs=out_specs,
        out_shape=out_shape,
        compiler_params=_params("parallel"),
        name="matmul_residual_ln",
    )(a1, a2, w, resid, g, b)


def _moba_select_blocks(q_ref, k_ref, sel_ref, n_blocks):
    bs = MOBA_BLOCK
    seq, dh = q_ref.shape
    kmean = jnp.sum(k_ref[...].reshape(n_blocks, bs, dh), axis=1) * (1.0 / bs)
    gate = lax.dot_general(kmean, q_ref[...], _NT, precision=lax.Precision.HIGHEST,
                           preferred_element_type=_F32)
    blk = lax.broadcasted_iota(jnp.int32, gate.shape, 0)
    tok = lax.broadcasted_iota(jnp.int32, gate.shape, 1)
    gate = jnp.where(tok >= (blk + 1) * bs, gate, NEG_INF)
    rank = jnp.zeros(gate.shape, _F32)
    for j in range(n_blocks):
        gj = gate[j:j + 1, :]
        ahead = jnp.logical_or(gj > gate, jnp.logical_and(gj == gate, blk > j))
        rank = rank + jnp.where(ahead, 1.0, 0.0)
    picked = jnp.logical_and(rank < MOBA_TOPK, gate > 0.5 * NEG_INF)
    sel_t = jnp.where(picked, 1.0, 0.0)
    sel_t = jnp.concatenate([sel_t, jnp.zeros((V7X_LANES - n_blocks, seq), _F32)], axis=0)
    sel_ref[...] = sel_t.T


def _moba_block_body(q_ref, kb_ref, vb_ref, o_ref, sel_ref, n_past):
    bs = MOBA_BLOCK
    dh = q_ref.shape[1]
    n_keys = (n_past + 1) * bs
    own = slice(n_past * bs, n_keys)
    qb = (q_ref[own, :] * (dh ** -0.5)).astype(_BF16)
    row = lax.broadcasted_iota(jnp.int32, (bs, bs), 0)
    col = lax.broadcasted_iota(jnp.int32, (bs, bs), 1)
    parts = []
    if n_past > 0:
        sel = sel_ref[own, :]
        for j in range(n_past):
            s = lax.dot_general(qb, kb_ref[j * bs:(j + 1) * bs, :], _NT,
                                preferred_element_type=_F32)
            parts.append(jnp.where(sel[:, j:j + 1] > 0.0, s, NEG_INF))
    s = lax.dot_general(qb, kb_ref[own, :], _NT, preferred_element_type=_F32)
    parts.append(jnp.where(col <= row, s, NEG_INF))
    s_all = jnp.concatenate(parts, axis=-1)
    m = jnp.max(s_all, axis=-1, keepdims=True)
    p = jnp.exp(s_all - m)
    l = jnp.sum(p, axis=-1, keepdims=True)
    o = jnp.dot(p.astype(_BF16), vb_ref[0:n_keys, :], preferred_element_type=_F32)
    o_ref[own, :] = (o / l).astype(o_ref.dtype)


def _moba_kernel(q_ref, k_ref, v_ref, o_ref, sel_ref, kb_ref, vb_ref, *, n_blocks):
    if n_blocks > 1:
        _moba_select_blocks(q_ref, k_ref, sel_ref, n_blocks)
    kb_ref[...] = k_ref[...].astype(_BF16)
    vb_ref[...] = v_ref[...].astype(_BF16)
    for c in range(n_blocks):
        _moba_block_body(q_ref, kb_ref, vb_ref, o_ref, sel_ref, c)


def _moba(proj, *, batch, seq, q_col, k_col, v_col):
    nb = seq // MOBA_BLOCK
    dh = MOBA_HEAD_DIM
    return pl.pallas_call(
        functools.partial(_moba_kernel, n_blocks=nb),
        grid=(batch, MOBA_HEADS),
        in_specs=[pl.BlockSpec((seq, dh), lambda b, h: (b, q_col + h)),
                  pl.BlockSpec((seq, dh), lambda b, h: (b, k_col + h)),
                  pl.BlockSpec((seq, dh), lambda b, h: (b, v_col + h))],
        out_specs=pl.BlockSpec((seq, dh), lambda b, h: (b, h)),
        out_shape=jax.ShapeDtypeStruct((batch * seq, MOBA_HEADS * dh), _BF16),
        scratch_shapes=[pltpu.VMEM((seq, V7X_LANES), _F32),
                        pltpu.VMEM((seq, dh), _BF16),
                        pltpu.VMEM((seq, dh), _BF16)],
        compiler_params=_params("parallel", "parallel"),
        name="moba",
    )(proj, proj, proj)


GLA_GROUP = 4
GLA_UNROLL = 4


def _log_sigmoid(z):
    return jnp.minimum(z, 0.0) - jnp.log(1.0 + jnp.exp(-jnp.abs(z)))


def _gla_kernel(q_ref, k_ref, v_ref, r_ref, lr_ref, up_ref, bias_ref, g_ref, o_ref,
                la_ref, qd_ref, ks_ref, oi_ref, dec_ref, *, n_chunks):
    c = GLA_CHUNK
    dk = q_ref.shape[1]
    dv = v_ref.shape[1]
    sub = V7X_SUBLANES
    gain = g_ref[...]
    hi = lax.Precision.HIGHEST

    z = jnp.dot(lr_ref[...], up_ref[...], precision=hi, preferred_element_type=_F32)
    la_ref[...] = _log_sigmoid(z + bias_ref[...]) * (1.0 / GLA_GATE_TAU)

    grp = GLA_GROUP
    gr = grp * c
    row = lax.broadcasted_iota(jnp.int32, (gr, gr), 0)
    col = lax.broadcasted_iota(jnp.int32, (gr, gr), 1)
    shift = c.bit_length() - 1
    causal = jnp.logical_and(jnp.right_shift(row, shift) == jnp.right_shift(col, shift), col <= row)
    causal_f = causal.astype(_F32)
    for g in range(n_chunks // grp):
        rows = slice(g * gr, (g + 1) * gr)
        cum = jnp.dot(causal_f, la_ref[rows, :], precision=hi, preferred_element_type=_F32)
        cum_last = jnp.concatenate(
            [jnp.broadcast_to(cum[(u + 1) * c - 1:(u + 1) * c, :], (c, dk)) for u in range(grp)],
            axis=0)
        q = q_ref[rows, :] * (dk ** -0.5)
        k = k_ref[rows, :]
        q_dec = (q * jnp.exp(cum)).astype(_BF16)
        k_dec = (k * jnp.exp(-cum)).astype(_BF16)
        a = lax.dot_general(q_dec, k_dec, _NT, preferred_element_type=_F32)
        a = jnp.where(causal, a, 0.0).astype(_BF16)
        qd_ref[rows, :] = q_dec
        ks_ref[rows, :] = (k * jnp.exp(cum_last - cum)).astype(_BF16)
        oi_ref[rows, :] = jnp.dot(a, v_ref[rows, :].astype(_BF16), preferred_element_type=_F32)
        for u in range(grp):
            n = g * grp + u
            dec_ref[n * sub:(n + 1) * sub, :] = jnp.exp(cum_last[u * c:u * c + sub, :])

    def inter(n, state_t):
        rows = pl.ds(pl.multiple_of(n * c, c), c)
        o = oi_ref[rows, :] + lax.dot_general(qd_ref[rows, :], state_t.astype(_BF16), _NT,
                                              preferred_element_type=_F32)
        delta_t = lax.dot_general(v_ref[rows, :].astype(_BF16), ks_ref[rows, :], _TN,
                                  preferred_element_type=_F32)
        decay = dec_ref[pl.ds(pl.multiple_of(n * sub, sub), 1), :]
        o = o * lax.rsqrt(jnp.mean(o * o, axis=-1, keepdims=True) + RMS_EPS) * gain
        r = r_ref[rows, :]
        o_ref[rows, :] = (o * (r * jax.nn.sigmoid(r))).astype(o_ref.dtype)
        return state_t * decay + delta_t

    lax.fori_loop(0, n_chunks, inter, jnp.zeros((dv, dk), _F32), unroll=GLA_UNROLL)


def _gla(proj, gate_up, gate_bias, norm_g, *, batch, seq, q_col, k_col, v_col, r_col, lr_col):
    dk, dv = GLA_KEY_DIM, GLA_VAL_DIM
    return pl.pallas_call(
        functools.partial(_gla_kernel, n_chunks=seq // GLA_CHUNK),
        grid=(batch, GLA_HEADS),
        in_specs=[pl.BlockSpec((seq, dk), lambda b, h: (b, q_col + h)),
                  pl.BlockSpec((seq, dk), lambda b, h: (b, k_col + h)),
                  pl.BlockSpec((seq, dv), lambda b, h: (b, v_col + h)),
                  pl.BlockSpec((seq, dv), lambda b, h: (b, r_col + h)),
                  pl.BlockSpec((seq, V7X_LANES), lambda b, h: (b, lr_col)),
                  pl.BlockSpec((V7X_LANES, dk), lambda b, h: (0, h)),
                  pl.BlockSpec((1, dk), lambda b, h: (0, h)),
                  pl.BlockSpec((1, dv), lambda b, h: (0, 0))],
        out_specs=pl.BlockSpec((seq, dv), lambda b, h: (b, h)),
        out_shape=jax.ShapeDtypeStruct((batch * seq, GLA_HEADS * dv), _BF16),
        scratch_shapes=[pltpu.VMEM((seq, dk), _F32),
                        pltpu.VMEM((seq, dk), _BF16),
                        pltpu.VMEM((seq, dk), _BF16),
                        pltpu.VMEM((seq, dv), _F32),
                        pltpu.VMEM((seq // GLA_CHUNK * V7X_SUBLANES, dk), _F32)],
        compiler_params=_params("parallel", "parallel"),
        name="gla",
    )(proj, proj, proj, proj, proj, gate_up, gate_bias, norm_g)


def _xattn_kernel(q_ref, kv_ref, o_ref):
    d = q_ref.shape[1]
    hd = d // XATTN_HEADS
    scale = hd ** -0.5
    for h in range(XATTN_HEADS):
        qh = q_ref[:, h * hd:(h + 1) * hd]
        kh = kv_ref[:, h * hd:(h + 1) * hd]
        vh = kv_ref[:, d + h * hd:d + (h + 1) * hd]
        s = lax.dot_general(qh, kh, _NT, preferred_element_type=_F32) * scale
        m = jnp.max(s, axis=-1, keepdims=True)
        p = jnp.exp(s - m)
        l = jnp.sum(p, axis=-1, keepdims=True)
        o = jnp.dot(p.astype(_BF16), vh, preferred_element_type=_F32)
        o_ref[:, h * hd:(h + 1) * hd] = (o / l).astype(o_ref.dtype)


def _xattn(q, kv, *, batch, seq, mem_len, bq):
    d = q.shape[1]
    nq = seq // bq
    return pl.pallas_call(
        _xattn_kernel,
        grid=(batch, nq),
        in_specs=[pl.BlockSpec((bq, d), lambda b, i: (b * nq + i, 0)),
                  pl.BlockSpec((mem_len, 2 * d), lambda b, i: (b, 0))],
        out_specs=pl.BlockSpec((bq, d), lambda b, i: (b * nq + i, 0)),
        out_shape=jax.ShapeDtypeStruct((batch * seq, d), _BF16),
        compiler_params=_params("parallel", "parallel"),
        name="xattn",
    )(q, kv)


def _top_values(s, k, with_rank=False):
    tm = s.shape[1]
    rid = lax.broadcasted_iota(jnp.int32, (k, tm), 0)
    top = jnp.zeros((k, tm), _F32)
    rank = jnp.full(s.shape, float(k), _F32)
    for r in range(k):
        m = jnp.max(s, axis=0, keepdims=True)
        top = jnp.where(rid == r, m, top)
        hit = s == m
        if with_rank:
            rank = jnp.where(hit, float(r), rank)
        s = jnp.where(hit, -jnp.inf, s)
    return (top, rank) if with_rank else top


def _pair_bf16(x):
    u = pltpu.bitcast(x.astype(_BF16).astype(_F32), jnp.uint32)
    return u | lax.shift_right_logical(u, jnp.uint32(16))


def _peer_select_kernel(q_ref, keys_ref, cnt_ref, e1_ref, rank_ref, e2_ref):
    kk = PEER_TOPK
    dq = keys_ref.shape[3]
    hi = lax.Precision.HIGHEST
    for h in range(PEER_HEADS):
        q1 = q_ref[:, (2 * h) * dq:(2 * h + 1) * dq]
        q2 = q_ref[:, (2 * h + 1) * dq:(2 * h + 2) * dq]
        s1 = lax.dot_general(keys_ref[h, 0], q1, _NT, precision=hi,
                             preferred_element_type=_F32)
        s2 = lax.dot_general(keys_ref[h, 1], q2, _NT, precision=hi,
                             preferred_element_type=_F32)
        a = _top_values(s1, kk)
        b, rank2 = _top_values(s2, kk, with_rank=True)
        half = kk // 2
        groups = [a[0:1, :] + b[0:half, :], a[0:1, :] + b[half:kk, :]]
        groups += [a[r:r + 1, :] + b[0:half, :] for r in range(1, half)]
        groups += [a[half:kk, :] + b[0:1, :]]
        best = _top_values(jnp.concatenate(groups, axis=0), kk)
        tau = best[kk - 1:kk, :]
        a0 = a[0:1, :]
        b0 = b[0:1, :]
        z = jnp.sum(jnp.exp(best - (a0 + b0)), axis=0, keepdims=True)
        cnt1 = jnp.zeros(s1.shape, _F32)
        for c in range(kk):
            cnt1 = cnt1 + jnp.where(s1 + b[c:c + 1, :] >= tau, 1.0, 0.0)
        cnt_ref[h] = _pair_bf16(cnt1)
        e1_ref[h] = _pair_bf16(jnp.exp(s1 - a0))
        rank_ref[h] = rank2.astype(_BF16)
        e2_ref[h] = (jnp.exp(s2 - b0) / z).astype(_BF16)


def _peer_select(q, sub_keys, *, tm):
    t = q.shape[0]
    nk = PEER_N_KEYS
    spec = pl.BlockSpec((PEER_HEADS, nk, tm), lambda i: (0, 0, i))
    pair = jax.ShapeDtypeStruct((PEER_HEADS, nk, t), jnp.uint32)
    half = jax.ShapeDtypeStruct((PEER_HEADS, nk, t), _BF16)
    return pl.pallas_call(
        _peer_select_kernel,
        grid=(t // tm,),
        in_specs=[pl.BlockSpec((tm, q.shape[1]), lambda i: (i, 0)),
                  pl.BlockSpec(sub_keys.shape, lambda i: (0, 0, 0, 0))],
        out_specs=[spec, spec, spec, spec],
        out_shape=[pair, pair, half, half],
        compiler_params=_params("parallel"),
        name="peer_select",
    )(q, sub_keys)


PEER_ROWS_PER_CHUNK = V7X_SUBLANES
PEER_LANE_CHUNK = 2 * V7X_LANES


def _bf16_rows(pair_row):
    tile = jnp.broadcast_to(pair_row, (V7X_SUBLANES, pair_row.shape[1]))
    return pltpu.bitcast(tile, _BF16)[0:V7X_SUBLANES]


def _peer_experts_kernel(xt_ref, u_ref, vt_ref, cnt_ref, e1_ref, rank_ref, e2_ref,
                         r_ref, lng_ref, lnb_ref, o_ref, hdn_ref, g_ref, acc_ref):
    e = pl.program_id(1)
    nk = PEER_N_KEYS
    tm = xt_ref.shape[1]
    sl = V7X_SUBLANES
    lw = PEER_LANE_CHUNK

    @pl.when(e == 0)
    def _():
        acc_ref[...] = jnp.zeros_like(acc_ref)

    hdn_ref[...] = jnp.dot(u_ref[...], xt_ref[...], preferred_element_type=_F32)
    sqrt_half = math.sqrt(0.5)
    for lc in range(tm // lw):
        lanes = slice(lc * lw, (lc + 1) * lw)
        for ii in range(PEER_ROWS_PER_CHUNK):
            w = None
            for h in range(PEER_HEADS):
                cnt = _bf16_rows(cnt_ref[h, ii:ii + 1, lanes])[None]
                e1 = _bf16_rows(e1_ref[h, ii:ii + 1, lanes])[None]
                rank = rank_ref[h, :, lanes].reshape(nk // sl, sl, lw)
                e2 = e2_ref[h, :, lanes].reshape(nk // sl, sl, lw)
                term = jnp.where(rank < cnt, e2, jnp.zeros_like(e2)) * e1
                w = term if w is None else w + term
            hh = hdn_ref[ii * nk:(ii + 1) * nk, lanes]
            act = (0.5 * hh) * (1.0 + lax.erf(hh * sqrt_half))
            g_ref[ii * nk:(ii + 1) * nk, lanes] = w.reshape(nk, lw) * act.astype(_BF16)
    acc_ref[...] += jnp.dot(vt_ref[...], g_ref[...], preferred_element_type=_F32)

    @pl.when(e == pl.num_programs(1) - 1)
    def _():
        ff = acc_ref[...].T
        o_ref[...] = _layer_norm_rows(DEEPNORM_ALPHA * r_ref[...] + ff, lng_ref[...], lnb_ref[...])


def _peer_experts(xt, u, vt, cnt, e1, rank, e2, resid, ln_g, ln_b, *, tm):
    d, t = xt.shape
    n_exp = u.shape[0]
    nk = PEER_N_KEYS
    rows = PEER_ROWS_PER_CHUNK
    chunk = rows * nk
    return pl.pallas_call(
        _peer_experts_kernel,
        grid=(t // tm, n_exp // chunk),
        in_specs=[pl.BlockSpec((d, tm), lambda i, e: (0, i)),
                  pl.BlockSpec((chunk, d), lambda i, e: (e, 0)),
                  pl.BlockSpec((d, chunk), lambda i, e: (0, e)),
                  pl.BlockSpec((PEER_HEADS, rows, tm), lambda i, e: (0, e, i)),
                  pl.BlockSpec((PEER_HEADS, rows, tm), lambda i, e: (0, e, i)),
                  pl.BlockSpec((PEER_HEADS, nk, tm), lambda i, e: (0, 0, i)),
                  pl.BlockSpec((PEER_HEADS, nk, tm), lambda i, e: (0, 0, i)),
                  pl.BlockSpec((tm, d), lambda i, e: (i, 0)),
                  pl.BlockSpec((1, d), lambda i, e: (0, 0)),
                  pl.BlockSpec((1, d), lambda i, e: (0, 0))],
        out_specs=pl.BlockSpec((tm, d), lambda i, e: (i, 0)),
        out_shape=jax.ShapeDtypeStruct((t, d), _F32),
        scratch_shapes=[pltpu.VMEM((chunk, tm), _F32),
                        pltpu.VMEM((chunk, tm), _BF16),
                        pltpu.VMEM((d, tm), _F32)],
        compiler_params=_params("parallel", "arbitrary"),
        name="peer_experts",
    )(xt, u, vt, cnt, e1, rank, e2, resid, ln_g, ln_b)


def _layer(x, mem, w_in, gate_up, gate_bias, norm_g, w_out, ln1_g, ln1_b,
           wq, wkv, wo, ln2_g, ln2_b, peer_wq, sub_keys, peer_u, peer_v, ln3_g, ln3_b):
    batch, seq, d = x.shape
    mem_len = mem.shape[1]
    t = batch * seq
    x2 = x.reshape(t, d)

    moba_w = MOBA_HEADS * MOBA_HEAD_DIM
    gla_kw = GLA_HEADS * GLA_KEY_DIM
    gla_vw = GLA_HEADS * GLA_VAL_DIM
    cols = w_in.shape[1]
    cols_pad = -(-cols // V7X_LANES) * V7X_LANES
    w_in_p = jnp.pad(w_in, ((0, 0), (0, cols_pad - cols))).astype(_BF16)
    proj = _matmul(x2.astype(_BF16), w_in_p, bm=1024, bn=cols_pad // 7, out_dtype=_F32)

    q_m, k_m, v_m = 0, moba_w, 2 * moba_w
    q_g = 3 * moba_w
    k_g = q_g + gla_kw
    v_g = k_g + gla_kw
    r_g = v_g + gla_vw
    lr = r_g + gla_vw
    o_moba = _moba(proj, batch=batch, seq=seq, q_col=q_m // MOBA_HEAD_DIM,
                   k_col=k_m // MOBA_HEAD_DIM, v_col=v_m // MOBA_HEAD_DIM)
    gate_up_p = jnp.pad(gate_up, ((0, V7X_LANES - gate_up.shape[0]), (0, 0)))
    o_gla = _gla(proj, gate_up_p, gate_bias.reshape(1, -1), norm_g.reshape(1, -1),
                 batch=batch, seq=seq, q_col=q_g // GLA_KEY_DIM, k_col=k_g // GLA_KEY_DIM,
                 v_col=v_g // GLA_VAL_DIM, r_col=r_g // GLA_VAL_DIM, lr_col=lr // V7X_LANES)
    h1, h1_bf = _matmul_residual_ln(o_moba, 0, o_gla, 0, w_out.astype(_BF16), x2,
                                    ln1_g.reshape(1, -1), ln1_b.reshape(1, -1), bm=512)

    q_x = _matmul(h1_bf, wq.astype(_BF16), bm=1024, bn=1024, out_dtype=_BF16)
    kv = _matmul(mem.reshape(batch * mem_len, d).astype(_BF16), wkv.astype(_BF16),
                 bm=1024, bn=1024, out_dtype=_BF16)
    o_x = _xattn(q_x, kv, batch=batch, seq=seq, mem_len=mem_len, bq=512)
    h2, h2_bf, h2_bf_t = _matmul_residual_ln(
        o_x, 0, o_x, 1, wo.astype(_BF16), h1, ln2_g.reshape(1, -1), ln2_b.reshape(1, -1),
        bm=512, transposed_copy=True)

    q_p = _matmul(h2_bf, peer_wq.astype(_BF16), bm=1024, bn=1024, out_dtype=_F32)
    cnt, e1, rank, e2 = _peer_select(q_p, sub_keys, tm=256)
    y = _peer_experts(h2_bf_t, peer_u.astype(_BF16), peer_v.astype(_BF16).T,
                      cnt, e1, rank, e2, h2, ln3_g.reshape(1, -1), ln3_b.reshape(1, -1), tm=512)
    return y.reshape(batch, seq, d)


def kernel(x, mem, w_in, gla_gate_up, gla_gate_bias, gla_norm_g, w_out, ln1_g, ln1_b,
           xattn_wq, xattn_wkv, xattn_wo, ln2_g, ln2_b,
           peer_wq, peer_sub_keys, peer_u, peer_v, ln3_g, ln3_b):
    h = x
    for l in range(w_in.shape[0]):
        h = _layer(h, mem, w_in[l], gla_gate_up[l], gla_gate_bias[l], gla_norm_g[l], w_out[l],
                   ln1_g[l], ln1_b[l], xattn_wq[l], xattn_wkv[l], xattn_wo[l], ln2_g[l], ln2_b[l],
                   peer_wq[l], peer_sub_keys[l], peer_u[l], peer_v[l], ln3_g[l], ln3_b[l])
    return h
```

```python
import functools
import math

import jax
import jax.numpy as jnp
import numpy as np
from jax import lax
from jax.experimental import pallas as pl
from jax.experimental.pallas import tpu as pltpu

MOBA_HEADS = 8
MOBA_HEAD_DIM = 128
MOBA_BLOCK = 256
MOBA_TOPK = 3
GLA_HEADS = 4
GLA_KEY_DIM = 128
GLA_VAL_DIM = 256
GLA_GATE_RANK = 16
GLA_GATE_TAU = 16.0
GLA_CHUNK = 64
XATTN_HEADS = 4
PEER_HEADS = 8
PEER_N_KEYS = 128
PEER_TOPK = 16
DEPTH = 1
DEEPNORM_ALPHA = (2.0 * DEPTH) ** 0.25
LN_EPS = 1e-5
RMS_EPS = 1e-6
NEG_INF = -1e30

V7X_LANES = 128
V7X_SUBLANES = 8
V7X_VMEM_LIMIT_BYTES = 56 * 1024 * 1024
V7X_MXU_COLS = 256
PROJ_BLOCK_COLS = 5 * V7X_MXU_COLS

_BF16 = jnp.bfloat16
_F32 = jnp.float32
_NT = (((1,), (1,)), ((), ()))
_TN = (((0,), (0,)), ((), ()))


def _params(*semantics):
    return pltpu.CompilerParams(dimension_semantics=semantics,
                                vmem_limit_bytes=V7X_VMEM_LIMIT_BYTES)


def _matmul_kernel(a_ref, b_ref, o_ref, *maybe_abf_ref):
    if maybe_abf_ref:
        abf_ref, = maybe_abf_ref

        @pl.when(pl.program_id(1) == 0)
        def _():
            abf_ref[...] = a_ref[...].astype(_BF16)

        a = abf_ref[...]
    else:
        a = a_ref[...]
    o_ref[...] = jnp.dot(a, b_ref[...], preferred_element_type=_F32).astype(o_ref.dtype)


def _matmul(a, b, *, bm, bn, out_dtype):
    m, k = a.shape
    _, n = b.shape
    bm = min(bm, m)
    assert m % bm == 0 and n % bn == 0
    cast_in_kernel = a.dtype != _BF16
    return pl.pallas_call(
        _matmul_kernel,
        grid=(m // bm, n // bn),
        in_specs=[pl.BlockSpec((bm, k), lambda i, j: (i, 0)),
                  pl.BlockSpec((k, bn), lambda i, j: (0, j))],
        out_specs=pl.BlockSpec((bm, bn), lambda i, j: (i, j)),
        out_shape=jax.ShapeDtypeStruct((m, n), out_dtype),
        scratch_shapes=[pltpu.VMEM((bm, k), _BF16)] if cast_in_kernel else [],
        compiler_params=_params("parallel", "arbitrary" if cast_in_kernel else "parallel"),
        name="matmul",
    )(a, b)


def _layer_norm_rows(z, g, b):
    mu = jnp.mean(z, axis=-1, keepdims=True)
    zc = z - mu
    var = jnp.mean(zc * zc, axis=-1, keepdims=True)
    return zc * lax.rsqrt(var + LN_EPS) * g + b


def _matmul_residual_ln_kernel(a1_ref, a2_ref, w_ref, r_ref, g_ref, b_ref, o_ref, obf_ref,
                               *maybe_obt_ref):
    kh = a1_ref.shape[1]
    acc = jnp.dot(a1_ref[...], w_ref[:kh, :], preferred_element_type=_F32)
    acc = acc + jnp.dot(a2_ref[...], w_ref[kh:, :], preferred_element_type=_F32)
    y = _layer_norm_rows(DEEPNORM_ALPHA * r_ref[...] + acc, g_ref[...], b_ref[...])
    o_ref[...] = y
    obf_ref[...] = y.astype(_BF16)
    for obt_ref in maybe_obt_ref:
        obt_ref[...] = y.T.astype(_BF16)


def _matmul_residual_ln(a1, a1_col, a2, a2_col, w, resid, g, b, *, bm, transposed_copy=False):
    m, d = resid.shape
    k = w.shape[0]
    kh = k // 2
    out_specs = [pl.BlockSpec((bm, d), lambda i: (i, 0)), pl.BlockSpec((bm, d), lambda i: (i, 0))]
    out_shape = [jax.ShapeDtypeStruct((m, d), _F32), jax.ShapeDtypeStruct((m, d), _BF16)]
    if transposed_copy:
        out_specs.append(pl.BlockSpec((d, bm), lambda i: (0, i)))
        out_shape.append(jax.ShapeDtypeStruct((d, m), _BF16))
    return pl.pallas_call(
        _matmul_residual_ln_kernel,
        grid=(m // bm,),
        in_specs=[pl.BlockSpec((bm, kh), lambda i: (i, a1_col)),
                  pl.BlockSpec((bm, kh), lambda i: (i, a2_col)),
                  pl.BlockSpec((k, d), lambda i: (0, 0)),
                  pl.BlockSpec((bm, d), lambda i: (i, 0)),
                  pl.BlockSpec((1, d), lambda i: (0, 0)),
                  pl.BlockSpec((1, d), lambda i: (0, 0))],
        out_specs=out_specs,
        out_shape=out_shape,
        compiler_params=_params("parallel"),
        name="matmul_residual_ln",
    )(a1, a2, w, resid, g, b)


def _moba_select_blocks(q_ref, k_ref, sel_ref, n_blocks):
    bs = MOBA_BLOCK
    seq, dh = q_ref.shape
    kmean = jnp.sum(k_ref[...].reshape(n_blocks, bs, dh), axis=1) * (1.0 / bs)
    gate = lax.dot_general(kmean, q_ref[...], _NT, precision=lax.Precision.HIGHEST,
                           preferred_element_type=_F32)
    blk = lax.broadcasted_iota(jnp.int32, gate.shape, 0)
    tok = lax.broadcasted_iota(jnp.int32, gate.shape, 1)
    gate = jnp.where(tok >= (blk + 1) * bs, gate, NEG_INF)
    rank = jnp.zeros(gate.shape, _F32)
    for j in range(n_blocks):
        gj = gate[j:j + 1, :]
        ahead = jnp.logical_or(gj > gate, jnp.logical_and(gj == gate, blk > j))
        rank = rank + jnp.where(ahead, 1.0, 0.0)
    picked = jnp.logical_and(rank < MOBA_TOPK, gate > 0.5 * NEG_INF)
    sel_t = jnp.where(picked, 1.0, 0.0)
    sel_t = jnp.concatenate([sel_t, jnp.zeros((V7X_LANES - n_blocks, seq), _F32)], axis=0)
    sel_ref[...] = sel_t.T


def _moba_block_body(q_ref, kb_ref, vb_ref, o_ref, sel_ref, n_past):
    bs = MOBA_BLOCK
    dh = q_ref.shape[1]
    n_keys = (n_past + 1) * bs
    own = slice(n_past * bs, n_keys)
    qb = (q_ref[own, :] * (dh ** -0.5)).astype(_BF16)
    row = lax.broadcasted_iota(jnp.int32, (bs, bs), 0)
    col = lax.broadcasted_iota(jnp.int32, (bs, bs), 1)
    parts = []
    if n_past > 0:
        sel = sel_ref[own, :]
        for j in range(n_past):
            s = lax.dot_general(qb, kb_ref[j * bs:(j + 1) * bs, :], _NT,
                                preferred_element_type=_F32)
            parts.append(jnp.where(sel[:, j:j + 1] > 0.0, s, NEG_INF))
    s = lax.dot_general(qb, kb_ref[own, :], _NT, preferred_element_type=_F32)
    parts.append(jnp.where(col <= row, s, NEG_INF))
    s_all = jnp.concatenate(parts, axis=-1)
    m = jnp.max(s_all, axis=-1, keepdims=True)
    p = jnp.exp(s_all - m)
    l = jnp.sum(p, axis=-1, keepdims=True)
    o = jnp.dot(p.astype(_BF16), vb_ref[0:n_keys, :], preferred_element_type=_F32)
    o_ref[own, :] = (o / l).astype(o_ref.dtype)


def _moba_kernel(q_ref, k_ref, v_ref, o_ref, sel_ref, kb_ref, vb_ref, *, n_blocks):
    if n_blocks > 1:
        _moba_select_blocks(q_ref, k_ref, sel_ref, n_blocks)
    kb_ref[...] = k_ref[...].astype(_BF16)
    vb_ref[...] = v_ref[...].astype(_BF16)
    for c in range(n_blocks):
        _moba_block_body(q_ref, kb_ref, vb_ref, o_ref, sel_ref, c)


def _moba(proj, *, batch, seq, q_col, k_col, v_col):
    nb = seq // MOBA_BLOCK
    dh = MOBA_HEAD_DIM
    return pl.pallas_call(
        functools.partial(_moba_kernel, n_blocks=nb),
        grid=(batch, MOBA_HEADS),
        in_specs=[pl.BlockSpec((seq, dh), lambda b, h: (b, q_col + h)),
                  pl.BlockSpec((seq, dh), lambda b, h: (b, k_col + h)),
                  pl.BlockSpec((seq, dh), lambda b, h: (b, v_col + h))],
        out_specs=pl.BlockSpec((seq, dh), lambda b, h: (b, h)),
        out_shape=jax.ShapeDtypeStruct((batch * seq, MOBA_HEADS * dh), _BF16),
        scratch_shapes=[pltpu.VMEM((seq, V7X_LANES), _F32),
                        pltpu.VMEM((seq, dh), _BF16),
                        pltpu.VMEM((seq, dh), _BF16)],
        compiler_params=_params("parallel", "parallel"),
        name="moba",
    )(proj, proj, proj)


GLA_GROUP = 4
GLA_UNROLL = 4


def _log_sigmoid(z):
    return jnp.minimum(z, 0.0) - jnp.log(1.0 + jnp.exp(-jnp.abs(z)))


def _gla_kernel(q_ref, k_ref, v_ref, r_ref, lr_ref, up_ref, bias_ref, g_ref, o_ref,
                la_ref, qd_ref, ks_ref, oi_ref, dec_ref, *, n_chunks):
    c = GLA_CHUNK
    dk = q_ref.shape[1]
    dv = v_ref.shape[1]
    sub = V7X_SUBLANES
    gain = g_ref[...]
    hi = lax.Precision.HIGHEST

    z = jnp.dot(lr_ref[...], up_ref[...], precision=hi, preferred_element_type=_F32)
    la_ref[...] = _log_sigmoid(z + bias_ref[...]) * (1.0 / GLA_GATE_TAU)

    grp = GLA_GROUP
    gr = grp * c
    row = lax.broadcasted_iota(jnp.int32, (gr, gr), 0)
    col = lax.broadcasted_iota(jnp.int32, (gr, gr), 1)
    shift = c.bit_length() - 1
    causal = jnp.logical_and(jnp.right_shift(row, shift) == jnp.right_shift(col, shift), col <= row)
    causal_f = causal.astype(_F32)
    for g in range(n_chunks // grp):
        rows = slice(g * gr, (g + 1) * gr)
        cum = jnp.dot(causal_f, la_ref[rows, :], precision=hi, preferred_element_type=_F32)
        cum_last = jnp.concatenate(
            [jnp.broadcast_to(cum[(u + 1) * c - 1:(u + 1) * c, :], (c, dk)) for u in range(grp)],
            axis=0)
        q = q_ref[rows, :] * (dk ** -0.5)
        k = k_ref[rows, :]
        q_dec = (q * jnp.exp(cum)).astype(_BF16)
        k_dec = (k * jnp.exp(-cum)).astype(_BF16)
        a = lax.dot_general(q_dec, k_dec, _NT, preferred_element_type=_F32)
        a = jnp.where(causal, a, 0.0).astype(_BF16)
        qd_ref[rows, :] = q_dec
        ks_ref[rows, :] = (k * jnp.exp(cum_last - cum)).astype(_BF16)
        oi_ref[rows, :] = jnp.dot(a, v_ref[rows, :].astype(_BF16), preferred_element_type=_F32)
        for u in range(grp):
            n = g * grp + u
            dec_ref[n * sub:(n + 1) * sub, :] = jnp.exp(cum_last[u * c:u * c + sub, :])

    def inter(n, state_t):
        rows = pl.ds(pl.multiple_of(n * c, c), c)
        o = oi_ref[rows, :] + lax.dot_general(qd_ref[rows, :], state_t.astype(_BF16), _NT,
                                              preferred_element_type=_F32)
        delta_t = lax.dot_general(v_ref[rows, :].astype(_BF16), ks_ref[rows, :], _TN,
                                  preferred_element_type=_F32)
        decay = dec_ref[pl.ds(pl.multiple_of(n * sub, sub), 1), :]
        o = o * lax.rsqrt(jnp.mean(o * o, axis=-1, keepdims=True) + RMS_EPS) * gain
        r = r_ref[rows, :]
        o_ref[rows, :] = (o * (r * jax.nn.sigmoid(r))).astype(o_ref.dtype)
        return state_t * decay + delta_t

    lax.fori_loop(0, n_chunks, inter, jnp.zeros((dv, dk), _F32), unroll=GLA_UNROLL)


def _gla(proj, gate_up, gate_bias, norm_g, *, batch, seq, q_col, k_col, v_col, r_col, lr_col):
    dk, dv = GLA_KEY_DIM, GLA_VAL_DIM
    return pl.pallas_call(
        functools.partial(_gla_kernel, n_chunks=seq // GLA_CHUNK),
        grid=(batch, GLA_HEADS),
        in_specs=[pl.BlockSpec((seq, dk), lambda b, h: (b, q_col + h)),
                  pl.BlockSpec((seq, dk), lambda b, h: (b, k_col + h)),
                  pl.BlockSpec((seq, dv), lambda b, h: (b, v_col + h)),
                  pl.BlockSpec((seq, dv), lambda b, h: (b, r_col + h)),
                  pl.BlockSpec((seq, V7X_LANES), lambda b, h: (b, lr_col)),
                  pl.BlockSpec((V7X_LANES, dk), lambda b, h: (0, h)),
                  pl.BlockSpec((1, dk), lambda b, h: (0, h)),
                  pl.BlockSpec((1, dv), lambda b, h: (0, 0))],
        out_specs=pl.BlockSpec((seq, dv), lambda b, h: (b, h)),
        out_shape=jax.ShapeDtypeStruct((batch * seq, GLA_HEADS * dv), _BF16),
        scratch_shapes=[pltpu.VMEM((seq, dk), _F32),
                        pltpu.VMEM((seq, dk), _BF16),
                        pltpu.VMEM((seq, dk), _BF16),
                        pltpu.VMEM((seq, dv), _F32),
                        pltpu.VMEM((seq // GLA_CHUNK * V7X_SUBLANES, dk), _F32)],
        compiler_params=_params("parallel", "parallel"),
        name="gla",
    )(proj, proj, proj, proj, proj, gate_up, gate_bias, norm_g)


def _xattn_kernel(q_ref, kv_ref, o_ref):
    d = q_ref.shape[1]
    hd = d // XATTN_HEADS
    scale = hd ** -0.5
    for h in range(XATTN_HEADS):
        qh = q_ref[:, h * hd:(h + 1) * hd]
        kh = kv_ref[:, h * hd:(h + 1) * hd]
        vh = kv_ref[:, d + h * hd:d + (h + 1) * hd]
        s = lax.dot_general(qh, kh, _NT, preferred_element_type=_F32) * scale
        m = jnp.max(s, axis=-1, keepdims=True)
        p = jnp.exp(s - m)
        l = jnp.sum(p, axis=-1, keepdims=True)
        o = jnp.dot(p.astype(_BF16), vh, preferred_element_type=_F32)
        o_ref[:, h * hd:(h + 1) * hd] = (o / l).astype(o_ref.dtype)


def _xattn(q, kv, *, batch, seq, mem_len, bq):
    d = q.shape[1]
    nq = seq // bq
    return pl.pallas_call(
        _xattn_kernel,
        grid=(batch, nq),
        in_specs=[pl.BlockSpec((bq, d), lambda b, i: (b * nq + i, 0)),
                  pl.BlockSpec((mem_len, 2 * d), lambda b, i: (b, 0))],
        out_specs=pl.BlockSpec((bq, d), lambda b, i: (b * nq + i, 0)),
        out_shape=jax.ShapeDtypeStruct((batch * seq, d), _BF16),
        compiler_params=_params("parallel", "parallel"),
        name="xattn",
    )(q, kv)


def _top_values(s, k):
    tm = s.shape[1]
    rid = lax.broadcasted_iota(jnp.int32, (k, tm), 0)
    top = jnp.zeros((k, tm), _F32)
    for r in range(k):
        m = jnp.max(s, axis=0, keepdims=True)
        top = jnp.where(rid == r, m, top)
        s = jnp.where(s == m, -jnp.inf, s)
    return top


def _sorting_network(n):
    pairs = []
    p = 1
    while p < n:
        k = p
        while k >= 1:
            for j in range(k % p, n - k, 2 * k):
                for i in range(min(k, n - j - k)):
                    if (i + j) // (2 * p) == (i + j + k) // (2 * p):
                        pairs.append((i + j, i + j + k))
            k //= 2
        p *= 2
    return pairs


def _compare_exchange(c, i, j):
    c[i], c[j] = jnp.maximum(c[i], c[j]), jnp.minimum(c[i], c[j])


def _top16_sorted(groups):
    k = len(groups)
    c = list(groups)
    for i, j in _sorting_network(k):
        _compare_exchange(c, i, j)
    shift = V7X_SUBLANES // 2
    while shift >= 1:
        p = [pltpu.roll(x, shift, axis=0) for x in c]
        c = [jnp.maximum(c[i], p[k - 1 - i]) for i in range(k)]
        d = k // 2
        while d >= 1:
            for i in range(k):
                if not i & d:
                    _compare_exchange(c, i, i + d)
            d //= 2
        shift //= 2
    return c


def _prefix_count(vals, pred):
    g1 = pred(vals[7])
    g2 = pred(jnp.where(g1, vals[11], vals[3]))
    g3 = pred(jnp.where(g1, jnp.where(g2, vals[13], vals[9]), jnp.where(g2, vals[5], vals[1])))
    lo = jnp.where(g2, jnp.where(g3, vals[6], vals[4]), jnp.where(g3, vals[2], vals[0]))
    hi = jnp.where(g2, jnp.where(g3, vals[14], vals[12]), jnp.where(g3, vals[10], vals[8]))
    g4 = pred(jnp.where(g1, hi, lo))
    g5 = pred(vals[15])
    return (jnp.where(g1, 8.0, 0.0) + jnp.where(g2, 4.0, 0.0) + jnp.where(g3, 2.0, 0.0)
            + jnp.where(g4, 1.0, 0.0) + jnp.where(g5, 1.0, 0.0))


def _stack_rows(reps):
    sub = lax.broadcasted_iota(jnp.int32, reps[0].shape, 0)
    out = reps[0]
    for d in range(1, len(reps)):
        out = jnp.where(sub == d, reps[d], out)
    return out


def _pair_bf16(x):
    u = pltpu.bitcast(x.astype(_BF16).astype(_F32), jnp.uint32)
    return u | lax.shift_right_logical(u, jnp.uint32(16))


def _peer_select_kernel(q_ref, keys_ref, cnt_ref, e1_ref, rank_ref, e2_ref):
    kk = PEER_TOPK
    sub = V7X_SUBLANES
    dq = keys_ref.shape[3]
    n_groups = keys_ref.shape[2] // sub
    assert kk == 16 and n_groups == 16
    hi = lax.Precision.HIGHEST
    for h in range(PEER_HEADS):
        q1 = q_ref[:, (2 * h) * dq:(2 * h + 1) * dq]
        q2 = q_ref[:, (2 * h + 1) * dq:(2 * h + 2) * dq]
        s1 = lax.dot_general(keys_ref[h, 0], q1, _NT, precision=hi,
                             preferred_element_type=_F32)
        s2 = lax.dot_general(keys_ref[h, 1], q2, _NT, precision=hi,
                             preferred_element_type=_F32)
        g1 = [s1[r * sub:(r + 1) * sub, :] for r in range(n_groups)]
        g2 = [s2[r * sub:(r + 1) * sub, :] for r in range(n_groups)]
        a = _top16_sorted(g1)
        b = _top16_sorted(g2)
        a16 = jnp.concatenate([_stack_rows(a[:sub]), _stack_rows(a[sub:])], axis=0)
        b16 = jnp.concatenate([_stack_rows(b[:sub]), _stack_rows(b[sub:])], axis=0)
        half = kk // 2
        cells = [a16[0:1, :] + b16[0:half, :], a16[0:1, :] + b16[half:kk, :]]
        cells += [a16[r:r + 1, :] + b16[0:half, :] for r in range(1, half)]
        cells += [a16[half:kk, :] + b16[0:1, :]]
        best = _top_values(jnp.concatenate(cells, axis=0), kk)
        tau = jnp.broadcast_to(best[kk - 1:kk, :], a[0].shape)
        z = jnp.sum(jnp.exp(best - (a16[0:1, :] + b16[0:1, :])), axis=0, keepdims=True)
        inv_z = 1.0 / z
        cnt1, e1, rank2, e2 = [], [], [], []
        for r in range(n_groups):
            x1, x2 = g1[r], g2[r]
            cnt1.append(_prefix_count(b, lambda piv: x1 + piv >= tau))
            rank2.append(_prefix_count(b, lambda piv: piv > x2))
            e1.append(jnp.exp(x1 - a[0]))
            e2.append(jnp.exp(x2 - b[0]) * inv_z)
        cnt_ref[h] = _pair_bf16(jnp.concatenate(cnt1, axis=0))
        e1_ref[h] = _pair_bf16(jnp.concatenate(e1, axis=0))
        rank_ref[h] = jnp.concatenate(rank2, axis=0).astype(_BF16)
        e2_ref[h] = jnp.concatenate(e2, axis=0).astype(_BF16)


def _peer_select(q, sub_keys, *, tm):
    t = q.shape[0]
    nk = PEER_N_KEYS
    spec = pl.BlockSpec((PEER_HEADS, nk, tm), lambda i: (0, 0, i))
    pair = jax.ShapeDtypeStruct((PEER_HEADS, nk, t), jnp.uint32)
    half = jax.ShapeDtypeStruct((PEER_HEADS, nk, t), _BF16)
    return pl.pallas_call(
        _peer_select_kernel,
        grid=(t // tm,),
        in_specs=[pl.BlockSpec((tm, q.shape[1]), lambda i: (i, 0)),
                  pl.BlockSpec(sub_keys.shape, lambda i: (0, 0, 0, 0))],
        out_specs=[spec, spec, spec, spec],
        out_shape=[pair, pair, half, half],
        compiler_params=_params("parallel"),
        name="peer_select",
    )(q, sub_keys)


PEER_ROWS_PER_CHUNK = V7X_SUBLANES
PEER_LANE_CHUNK = 2 * V7X_LANES


def _bf16_rows(pair_row):
    tile = jnp.broadcast_to(pair_row, (V7X_SUBLANES, pair_row.shape[1]))
    return pltpu.bitcast(tile, _BF16)[0:V7X_SUBLANES]


def _peer_experts_kernel(xt_ref, u_ref, vt_ref, cnt_ref, e1_ref, rank_ref, e2_ref,
                         r_ref, lng_ref, lnb_ref, o_ref, hdn_ref, g_ref, acc_ref):
    e = pl.program_id(1)
    nk = PEER_N_KEYS
    tm = xt_ref.shape[1]
    sl = V7X_SUBLANES
    lw = PEER_LANE_CHUNK

    @pl.when(e == 0)
    def _():
        acc_ref[...] = jnp.zeros_like(acc_ref)

    hdn_ref[...] = jnp.dot(u_ref[...], xt_ref[...], preferred_element_type=_F32)
    sqrt_half = math.sqrt(0.5)
    for lc in range(tm // lw):
        lanes = slice(lc * lw, (lc + 1) * lw)
        for ii in range(PEER_ROWS_PER_CHUNK):
            w = None
            for h in range(PEER_HEADS):
                cnt = _bf16_rows(cnt_ref[h, ii:ii + 1, lanes])[None]
                e1 = _bf16_rows(e1_ref[h, ii:ii + 1, lanes])[None]
                rank = rank_ref[h, :, lanes].reshape(nk // sl, sl, lw)
                e2 = e2_ref[h, :, lanes].reshape(nk // sl, sl, lw)
                term = jnp.where(rank < cnt, e2, jnp.zeros_like(e2)) * e1
                w = term if w is None else w + term
            hh = hdn_ref[ii * nk:(ii + 1) * nk, lanes]
            act = (0.5 * hh) * (1.0 + lax.erf(hh * sqrt_half))
            g_ref[ii * nk:(ii + 1) * nk, lanes] = w.reshape(nk, lw) * act.astype(_BF16)
    acc_ref[...] += jnp.dot(vt_ref[...], g_ref[...], preferred_element_type=_F32)

    @pl.when(e == pl.num_programs(1) - 1)
    def _():
        ff = acc_ref[...].T
        o_ref[...] = _layer_norm_rows(DEEPNORM_ALPHA * r_ref[...] + ff, lng_ref[...], lnb_ref[...])


def _peer_experts(xt, u, vt, cnt, e1, rank, e2, resid, ln_g, ln_b, *, tm):
    d, t = xt.shape
    n_exp = u.shape[0]
    nk = PEER_N_KEYS
    rows = PEER_ROWS_PER_CHUNK
    chunk = rows * nk
    return pl.pallas_call(
        _peer_experts_kernel,
        grid=(t // tm, n_exp // chunk),
        in_specs=[pl.BlockSpec((d, tm), lambda i, e: (0, i)),
                  pl.BlockSpec((chunk, d), lambda i, e: (e, 0)),
                  pl.BlockSpec((d, chunk), lambda i, e: (0, e)),
                  pl.BlockSpec((PEER_HEADS, rows, tm), lambda i, e: (0, e, i)),
                  pl.BlockSpec((PEER_HEADS, rows, tm), lambda i, e: (0, e, i)),
                  pl.BlockSpec((PEER_HEADS, nk, tm), lambda i, e: (0, 0, i)),
                  pl.BlockSpec((PEER_HEADS, nk, tm), lambda i, e: (0, 0, i)),
                  pl.BlockSpec((tm, d), lambda i, e: (i, 0)),
                  pl.BlockSpec((1, d), lambda i, e: (0, 0)),
                  pl.BlockSpec((1, d), lambda i, e: (0, 0))],
        out_specs=pl.BlockSpec((tm, d), lambda i, e: (i, 0)),
        out_shape=jax.ShapeDtypeStruct((t, d), _F32),
        scratch_shapes=[pltpu.VMEM((chunk, tm), _F32),
                        pltpu.VMEM((chunk, tm), _BF16),
                        pltpu.VMEM((d, tm), _F32)],
        compiler_params=_params("parallel", "arbitrary"),
        name="peer_experts",
    )(xt, u, vt, cnt, e1, rank, e2, resid, ln_g, ln_b)


def _layer(x, mem, w_in, gate_up, gate_bias, norm_g, w_out, ln1_g, ln1_b,
           wq, wkv, wo, ln2_g, ln2_b, peer_wq, sub_keys, peer_u, peer_v, ln3_g, ln3_b):
    batch, seq, d = x.shape
    mem_len = mem.shape[1]
    t = batch * seq
    x2 = x.reshape(t, d)

    moba_w = MOBA_HEADS * MOBA_HEAD_DIM
    gla_kw = GLA_HEADS * GLA_KEY_DIM
    gla_vw = GLA_HEADS * GLA_VAL_DIM
    cols = w_in.shape[1]
    cols_pad = -(-cols // PROJ_BLOCK_COLS) * PROJ_BLOCK_COLS
    w_in_p = jnp.pad(w_in, ((0, 0), (0, cols_pad - cols))).astype(_BF16)
    proj = _matmul(x2, w_in_p, bm=1024, bn=PROJ_BLOCK_COLS, out_dtype=_F32)

    q_m, k_m, v_m = 0, moba_w, 2 * moba_w
    q_g = 3 * moba_w
    k_g = q_g + gla_kw
    v_g = k_g + gla_kw
    r_g = v_g + gla_vw
    lr = r_g + gla_vw
    o_moba = _moba(proj, batch=batch, seq=seq, q_col=q_m // MOBA_HEAD_DIM,
                   k_col=k_m // MOBA_HEAD_DIM, v_col=v_m // MOBA_HEAD_DIM)
    gate_up_p = jnp.pad(gate_up, ((0, V7X_LANES - gate_up.shape[0]), (0, 0)))
    o_gla = _gla(proj, gate_up_p, gate_bias.reshape(1, -1), norm_g.reshape(1, -1),
                 batch=batch, seq=seq, q_col=q_g // GLA_KEY_DIM, k_col=k_g // GLA_KEY_DIM,
                 v_col=v_g // GLA_VAL_DIM, r_col=r_g // GLA_VAL_DIM, lr_col=lr // V7X_LANES)
    h1, h1_bf = _matmul_residual_ln(o_moba, 0, o_gla, 0, w_out.astype(_BF16), x2,
                                    ln1_g.reshape(1, -1), ln1_b.reshape(1, -1), bm=512)

    q_x = _matmul(h1_bf, wq.astype(_BF16), bm=1024, bn=1024, out_dtype=_BF16)
    kv = _matmul(mem.reshape(batch * mem_len, d), wkv.astype(_BF16),
                 bm=1024, bn=1024, out_dtype=_BF16)
    o_x = _xattn(q_x, kv, batch=batch, seq=seq, mem_len=mem_len, bq=512)
    h2, h2_bf, h2_bf_t = _matmul_residual_ln(
        o_x, 0, o_x, 1, wo.astype(_BF16), h1, ln2_g.reshape(1, -1), ln2_b.reshape(1, -1),
        bm=512, transposed_copy=True)

    q_p = _matmul(h2_bf, peer_wq.astype(_BF16), bm=1024, bn=1024, out_dtype=_F32)
    cnt, e1, rank, e2 = _peer_select(q_p, sub_keys, tm=256)
    y = _peer_experts(h2_bf_t, peer_u.astype(_BF16), peer_v.astype(_BF16).T,
                      cnt, e1, rank, e2, h2, ln3_g.reshape(1, -1), ln3_b.reshape(1, -1), tm=512)
    return y.reshape(batch, seq, d)


def kernel(x, mem, w_in, gla_gate_up, gla_gate_bias, gla_norm_g, w_out, ln1_g, ln1_b,
           xattn_wq, xattn_wkv, xattn_wo, ln2_g, ln2_b,
           peer_wq, peer_sub_keys, peer_u, peer_v, ln3_g, ln3_b):
    h = x
    for l in range(w_in.shape[0]):
        h = _layer(h, mem, w_in[l], gla_gate_up[l], gla_gate_bias[l], gla_norm_g[l], w_out[l],
                   ln1_g[l], ln1_b[l], xattn_wq[l], xattn_wkv[l], xattn_wo[l], ln2_g[l], ln2_b[l],
                   peer_wq[l], peer_sub_keys[l], peer_u[l], peer_v[l], ln3_g[l], ln3_b[l])
    return h
```

```python
import functools
import math

import jax
import jax.numpy as jnp
import numpy as np
from jax import lax
from jax.experimental import pallas as pl
from jax.experimental.pallas import tpu as pltpu

MOBA_HEADS = 8
MOBA_HEAD_DIM = 128
MOBA_BLOCK = 256
MOBA_TOPK = 3
GLA_HEADS = 4
GLA_KEY_DIM = 128
GLA_VAL_DIM = 256
GLA_GATE_RANK = 16
GLA_GATE_TAU = 16.0
GLA_CHUNK = 64
XATTN_HEADS = 4
PEER_HEADS = 8
PEER_N_KEYS = 128
PEER_TOPK = 16
DEPTH = 1
DEEPNORM_ALPHA = (2.0 * DEPTH) ** 0.25
LN_EPS = 1e-5
RMS_EPS = 1e-6
NEG_INF = -1e30

V7X_LANES = 128
V7X_SUBLANES = 8
V7X_VMEM_LIMIT_BYTES = 56 * 1024 * 1024
V7X_MXU_COLS = 256


class _Blocks:
    matmul_rows = 1024
    matmul_cols = 4 * V7X_MXU_COLS
    proj_cols = 5 * V7X_MXU_COLS
    ln_rows = 512
    xattn_rows = 512
    peer_select_tokens = 2 * V7X_LANES
    peer_expert_tokens = 4 * V7X_LANES

_BF16 = jnp.bfloat16
_F32 = jnp.float32
_NT = (((1,), (1,)), ((), ()))
_TN = (((0,), (0,)), ((), ()))


def _params(*semantics):
    return pltpu.CompilerParams(dimension_semantics=semantics,
                                vmem_limit_bytes=V7X_VMEM_LIMIT_BYTES)


def _matmul_kernel(a_ref, b_ref, o_ref, *maybe_abf_ref):
    if maybe_abf_ref:
        abf_ref, = maybe_abf_ref

        @pl.when(pl.program_id(1) == 0)
        def _():
            abf_ref[...] = a_ref[...].astype(_BF16)

        a = abf_ref[...]
    else:
        a = a_ref[...]
    o_ref[...] = jnp.dot(a, b_ref[...], preferred_element_type=_F32).astype(o_ref.dtype)


def _matmul(a, b, *, bm, bn, out_dtype):
    m, k = a.shape
    _, n = b.shape
    bm = min(bm, m)
    assert m % bm == 0 and n % bn == 0
    cast_in_kernel = a.dtype != _BF16
    return pl.pallas_call(
        _matmul_kernel,
        grid=(m // bm, n // bn),
        in_specs=[pl.BlockSpec((bm, k), lambda i, j: (i, 0)),
                  pl.BlockSpec((k, bn), lambda i, j: (0, j))],
        out_specs=pl.BlockSpec((bm, bn), lambda i, j: (i, j)),
        out_shape=jax.ShapeDtypeStruct((m, n), out_dtype),
        scratch_shapes=[pltpu.VMEM((bm, k), _BF16)] if cast_in_kernel else [],
        compiler_params=_params("parallel", "arbitrary" if cast_in_kernel else "parallel"),
        name="matmul",
    )(a, b)


def _layer_norm_rows(z, g, b):
    mu = jnp.mean(z, axis=-1, keepdims=True)
    zc = z - mu
    var = jnp.mean(zc * zc, axis=-1, keepdims=True)
    return zc * lax.rsqrt(var + LN_EPS) * g + b


def _matmul_residual_ln_kernel(a1_ref, a2_ref, w_ref, r_ref, g_ref, b_ref, o_ref, obf_ref,
                               *maybe_obt_ref):
    kh = a1_ref.shape[1]
    acc = jnp.dot(a1_ref[...], w_ref[:kh, :], preferred_element_type=_F32)
    acc = acc + jnp.dot(a2_ref[...], w_ref[kh:, :], preferred_element_type=_F32)
    y = _layer_norm_rows(DEEPNORM_ALPHA * r_ref[...] + acc, g_ref[...], b_ref[...])
    o_ref[...] = y
    obf_ref[...] = y.astype(_BF16)
    for obt_ref in maybe_obt_ref:
        obt_ref[...] = y.T.astype(_BF16)


def _matmul_residual_ln(a1, a1_col, a2, a2_col, w, resid, g, b, *, bm, transposed_copy=False):
    m, d = resid.shape
    k = w.shape[0]
    kh = k // 2
    out_specs = [pl.BlockSpec((bm, d), lambda i: (i, 0)), pl.BlockSpec((bm, d), lambda i: (i, 0))]
    out_shape = [jax.ShapeDtypeStruct((m, d), _F32), jax.ShapeDtypeStruct((m, d), _BF16)]
    if transposed_copy:
        out_specs.append(pl.BlockSpec((d, bm), lambda i: (0, i)))
        out_shape.append(jax.ShapeDtypeStruct((d, m), _BF16))
    return pl.pallas_call(
        _matmul_residual_ln_kernel,
        grid=(m // bm,),
        in_specs=[pl.BlockSpec((bm, kh), lambda i: (i, a1_col)),
                  pl.BlockSpec((bm, kh), lambda i: (i, a2_col)),
                  pl.BlockSpec((k, d), lambda i: (0, 0)),
                  pl.BlockSpec((bm, d), lambda i: (i, 0)),
                  pl.BlockSpec((1, d), lambda i: (0, 0)),
                  pl.BlockSpec((1, d), lambda i: (0, 0))],
        out_specs=out_specs,
        out_shape=out_shape,
        compiler_params=_params("parallel"),
        name="matmul_residual_ln",
    )(a1, a2, w, resid, g, b)


def _moba_select_blocks(q_ref, k_ref, sel_ref, n_blocks):
    bs = MOBA_BLOCK
    seq, dh = q_ref.shape
    kmean = jnp.sum(k_ref[...].reshape(n_blocks, bs, dh), axis=1) * (1.0 / bs)
    gate = lax.dot_general(kmean, q_ref[...], _NT, precision=lax.Precision.HIGHEST,
                           preferred_element_type=_F32)
    blk = lax.broadcasted_iota(jnp.int32, gate.shape, 0)
    tok = lax.broadcasted_iota(jnp.int32, gate.shape, 1)
    gate = jnp.where(tok >= (blk + 1) * bs, gate, NEG_INF)
    rank = jnp.zeros(gate.shape, _F32)
    for j in range(n_blocks):
        gj = gate[j:j + 1, :]
        ahead = jnp.logical_or(gj > gate, jnp.logical_and(gj == gate, blk > j))
        rank = rank + jnp.where(ahead, 1.0, 0.0)
    picked = jnp.logical_and(rank < MOBA_TOPK, gate > 0.5 * NEG_INF)
    sel_t = jnp.where(picked, 1.0, 0.0)
    sel_t = jnp.concatenate([sel_t, jnp.zeros((V7X_LANES - n_blocks, seq), _F32)], axis=0)
    sel_ref[...] = sel_t.T


def _moba_block_body(q_ref, kb_ref, vb_ref, o_ref, sel_ref, n_past):
    bs = MOBA_BLOCK
    dh = q_ref.shape[1]
    n_keys = (n_past + 1) * bs
    own = slice(n_past * bs, n_keys)
    qb = (q_ref[own, :] * (dh ** -0.5)).astype(_BF16)
    row = lax.broadcasted_iota(jnp.int32, (bs, bs), 0)
    col = lax.broadcasted_iota(jnp.int32, (bs, bs), 1)
    parts = []
    if n_past > 0:
        sel = sel_ref[own, :]
        for j in range(n_past):
            s = lax.dot_general(qb, kb_ref[j * bs:(j + 1) * bs, :], _NT,
                                preferred_element_type=_F32)
            parts.append(jnp.where(sel[:, j:j + 1] > 0.0, s, NEG_INF))
    s = lax.dot_general(qb, kb_ref[own, :], _NT, preferred_element_type=_F32)
    parts.append(jnp.where(col <= row, s, NEG_INF))
    s_all = jnp.concatenate(parts, axis=-1)
    m = jnp.max(s_all, axis=-1, keepdims=True)
    p = jnp.exp(s_all - m)
    l = jnp.sum(p, axis=-1, keepdims=True)
    o = jnp.dot(p.astype(_BF16), vb_ref[0:n_keys, :], preferred_element_type=_F32)
    o_ref[own, :] = (o / l).astype(o_ref.dtype)


def _moba_kernel(q_ref, k_ref, v_ref, o_ref, sel_ref, kb_ref, vb_ref, *, n_blocks):
    if n_blocks > 1:
        _moba_select_blocks(q_ref, k_ref, sel_ref, n_blocks)
    kb_ref[...] = k_ref[...].astype(_BF16)
    vb_ref[...] = v_ref[...].astype(_BF16)
    for c in range(n_blocks):
        _moba_block_body(q_ref, kb_ref, vb_ref, o_ref, sel_ref, c)


def _moba(proj, *, batch, seq, q_col, k_col, v_col):
    nb = seq // MOBA_BLOCK
    dh = MOBA_HEAD_DIM
    return pl.pallas_call(
        functools.partial(_moba_kernel, n_blocks=nb),
        grid=(batch, MOBA_HEADS),
        in_specs=[pl.BlockSpec((seq, dh), lambda b, h: (b, q_col + h)),
                  pl.BlockSpec((seq, dh), lambda b, h: (b, k_col + h)),
                  pl.BlockSpec((seq, dh), lambda b, h: (b, v_col + h))],
        out_specs=pl.BlockSpec((seq, dh), lambda b, h: (b, h)),
        out_shape=jax.ShapeDtypeStruct((batch * seq, MOBA_HEADS * dh), _BF16),
        scratch_shapes=[pltpu.VMEM((seq, V7X_LANES), _F32),
                        pltpu.VMEM((seq, dh), _BF16),
                        pltpu.VMEM((seq, dh), _BF16)],
        compiler_params=_params("parallel", "parallel"),
        name="moba",
    )(proj, proj, proj)


GLA_GROUP = 4
GLA_UNROLL = 4


def _log_sigmoid(z):
    return jnp.minimum(z, 0.0) - jnp.log(1.0 + jnp.exp(-jnp.abs(z)))


def _gla_kernel(q_ref, k_ref, v_ref, r_ref, lr_ref, up_ref, bias_ref, g_ref, o_ref,
                la_ref, qd_ref, ks_ref, oi_ref, dec_ref, *, n_chunks):
    c = GLA_CHUNK
    dk = q_ref.shape[1]
    dv = v_ref.shape[1]
    sub = V7X_SUBLANES
    gain = g_ref[...]
    hi = lax.Precision.HIGHEST

    z = jnp.dot(lr_ref[...], up_ref[...], precision=hi, preferred_element_type=_F32)
    la_ref[...] = _log_sigmoid(z + bias_ref[...]) * (1.0 / GLA_GATE_TAU)

    grp = GLA_GROUP
    gr = grp * c
    row = lax.broadcasted_iota(jnp.int32, (gr, gr), 0)
    col = lax.broadcasted_iota(jnp.int32, (gr, gr), 1)
    shift = c.bit_length() - 1
    causal = jnp.logical_and(jnp.right_shift(row, shift) == jnp.right_shift(col, shift), col <= row)
    causal_f = causal.astype(_F32)
    for g in range(n_chunks // grp):
        rows = slice(g * gr, (g + 1) * gr)
        cum = jnp.dot(causal_f, la_ref[rows, :], precision=hi, preferred_element_type=_F32)
        cum_last = jnp.concatenate(
            [jnp.broadcast_to(cum[(u + 1) * c - 1:(u + 1) * c, :], (c, dk)) for u in range(grp)],
            axis=0)
        q = q_ref[rows, :] * (dk ** -0.5)
        k = k_ref[rows, :]
        q_dec = (q * jnp.exp(cum)).astype(_BF16)
        k_dec = (k * jnp.exp(-cum)).astype(_BF16)
        a = lax.dot_general(q_dec, k_dec, _NT, preferred_element_type=_F32)
        a = jnp.where(causal, a, 0.0).astype(_BF16)
        qd_ref[rows, :] = q_dec
        ks_ref[rows, :] = (k * jnp.exp(cum_last - cum)).astype(_BF16)
        oi_ref[rows, :] = jnp.dot(a, v_ref[rows, :].astype(_BF16), preferred_element_type=_F32)
        for u in range(grp):
            n = g * grp + u
            dec_ref[n * sub:(n + 1) * sub, :] = jnp.exp(cum_last[u * c:u * c + sub, :])

    def inter(n, state_t):
        rows = pl.ds(pl.multiple_of(n * c, c), c)
        o = oi_ref[rows, :] + lax.dot_general(qd_ref[rows, :], state_t.astype(_BF16), _NT,
                                              preferred_element_type=_F32)
        delta_t = lax.dot_general(v_ref[rows, :].astype(_BF16), ks_ref[rows, :], _TN,
                                  preferred_element_type=_F32)
        decay = dec_ref[pl.ds(pl.multiple_of(n * sub, sub), 1), :]
        o = o * lax.rsqrt(jnp.mean(o * o, axis=-1, keepdims=True) + RMS_EPS) * gain
        r = r_ref[rows, :]
        o_ref[rows, :] = (o * (r * jax.nn.sigmoid(r))).astype(o_ref.dtype)
        return state_t * decay + delta_t

    lax.fori_loop(0, n_chunks, inter, jnp.zeros((dv, dk), _F32), unroll=GLA_UNROLL)


def _gla(proj, gate_up, gate_bias, norm_g, *, batch, seq, q_col, k_col, v_col, r_col, lr_col):
    dk, dv = GLA_KEY_DIM, GLA_VAL_DIM
    return pl.pallas_call(
        functools.partial(_gla_kernel, n_chunks=seq // GLA_CHUNK),
        grid=(batch, GLA_HEADS),
        in_specs=[pl.BlockSpec((seq, dk), lambda b, h: (b, q_col + h)),
                  pl.BlockSpec((seq, dk), lambda b, h: (b, k_col + h)),
                  pl.BlockSpec((seq, dv), lambda b, h: (b, v_col + h)),
                  pl.BlockSpec((seq, dv), lambda b, h: (b, r_col + h)),
                  pl.BlockSpec((seq, V7X_LANES), lambda b, h: (b, lr_col)),
                  pl.BlockSpec((V7X_LANES, dk), lambda b, h: (0, h)),
                  pl.BlockSpec((1, dk), lambda b, h: (0, h)),
                  pl.BlockSpec((1, dv), lambda b, h: (0, 0))],
        out_specs=pl.BlockSpec((seq, dv), lambda b, h: (b, h)),
        out_shape=jax.ShapeDtypeStruct((batch * seq, GLA_HEADS * dv), _BF16),
        scratch_shapes=[pltpu.VMEM((seq, dk), _F32),
                        pltpu.VMEM((seq, dk), _BF16),
                        pltpu.VMEM((seq, dk), _BF16),
                        pltpu.VMEM((seq, dv), _F32),
                        pltpu.VMEM((seq // GLA_CHUNK * V7X_SUBLANES, dk), _F32)],
        compiler_params=_params("parallel", "parallel"),
        name="gla",
    )(proj, proj, proj, proj, proj, gate_up, gate_bias, norm_g)


def _xattn_kernel(q_ref, kv_ref, o_ref):
    d = q_ref.shape[1]
    hd = d // XATTN_HEADS
    scale = hd ** -0.5
    for h in range(XATTN_HEADS):
        qh = q_ref[:, h * hd:(h + 1) * hd]
        kh = kv_ref[:, h * hd:(h + 1) * hd]
        vh = kv_ref[:, d + h * hd:d + (h + 1) * hd]
        s = lax.dot_general(qh, kh, _NT, preferred_element_type=_F32) * scale
        m = jnp.max(s, axis=-1, keepdims=True)
        p = jnp.exp(s - m)
        l = jnp.sum(p, axis=-1, keepdims=True)
        o = jnp.dot(p.astype(_BF16), vh, preferred_element_type=_F32)
        o_ref[:, h * hd:(h + 1) * hd] = (o / l).astype(o_ref.dtype)


def _xattn(q, kv, *, batch, seq, mem_len, bq):
    d = q.shape[1]
    nq = seq // bq
    return pl.pallas_call(
        _xattn_kernel,
        grid=(batch, nq),
        in_specs=[pl.BlockSpec((bq, d), lambda b, i: (b * nq + i, 0)),
                  pl.BlockSpec((mem_len, 2 * d), lambda b, i: (b, 0))],
        out_specs=pl.BlockSpec((bq, d), lambda b, i: (b * nq + i, 0)),
        out_shape=jax.ShapeDtypeStruct((batch * seq, d), _BF16),
        compiler_params=_params("parallel", "parallel"),
        name="xattn",
    )(q, kv)


def _top_values(s, k):
    tm = s.shape[1]
    rid = lax.broadcasted_iota(jnp.int32, (k, tm), 0)
    top = jnp.zeros((k, tm), _F32)
    for r in range(k):
        m = jnp.max(s, axis=0, keepdims=True)
        top = jnp.where(rid == r, m, top)
        s = jnp.where(s == m, -jnp.inf, s)
    return top


def _sorting_network(n):
    pairs = []
    p = 1
    while p < n:
        k = p
        while k >= 1:
            for j in range(k % p, n - k, 2 * k):
                for i in range(min(k, n - j - k)):
                    if (i + j) // (2 * p) == (i + j + k) // (2 * p):
                        pairs.append((i + j, i + j + k))
            k //= 2
        p *= 2
    return pairs


def _compare_exchange(c, i, j):
    c[i], c[j] = jnp.maximum(c[i], c[j]), jnp.minimum(c[i], c[j])


def _top16_sorted(groups):
    k = len(groups)
    c = list(groups)
    for i, j in _sorting_network(k):
        _compare_exchange(c, i, j)
    shift = V7X_SUBLANES // 2
    while shift >= 1:
        p = [pltpu.roll(x, shift, axis=0) for x in c]
        c = [jnp.maximum(c[i], p[k - 1 - i]) for i in range(k)]
        d = k // 2
        while d >= 1:
            for i in range(k):
                if not i & d:
                    _compare_exchange(c, i, i + d)
            d //= 2
        shift //= 2
    return c


def _prefix_count(vals, pred):
    g1 = pred(vals[7])
    g2 = pred(jnp.where(g1, vals[11], vals[3]))
    g3 = pred(jnp.where(g1, jnp.where(g2, vals[13], vals[9]), jnp.where(g2, vals[5], vals[1])))
    lo = jnp.where(g2, jnp.where(g3, vals[6], vals[4]), jnp.where(g3, vals[2], vals[0]))
    hi = jnp.where(g2, jnp.where(g3, vals[14], vals[12]), jnp.where(g3, vals[10], vals[8]))
    g4 = pred(jnp.where(g1, hi, lo))
    g5 = pred(vals[15])
    return (jnp.where(g1, 8.0, 0.0) + jnp.where(g2, 4.0, 0.0) + jnp.where(g3, 2.0, 0.0)
            + jnp.where(g4, 1.0, 0.0) + jnp.where(g5, 1.0, 0.0))


def _stack_rows(reps):
    sub = lax.broadcasted_iota(jnp.int32, reps[0].shape, 0)
    out = reps[0]
    for d in range(1, len(reps)):
        out = jnp.where(sub == d, reps[d], out)
    return out


def _pair_bf16(x):
    u = pltpu.bitcast(x.astype(_BF16).astype(_F32), jnp.uint32)
    return u | lax.shift_right_logical(u, jnp.uint32(16))


def _topk_exact(s, k):
    rows, tm = s.shape
    row = lax.broadcasted_iota(jnp.int32, s.shape, 0).astype(_F32)
    rid = lax.broadcasted_iota(jnp.int32, (k, tm), 0)
    top = jnp.zeros((k, tm), _F32)
    pos = jnp.full(s.shape, float(k), _F32)
    for r in range(k):
        m = jnp.max(s, axis=0, keepdims=True)
        first = jnp.min(jnp.where(s == m, row, float(rows)), axis=0, keepdims=True)
        hit = row == first
        top = jnp.where(rid == r, m, top)
        pos = jnp.where(hit, float(r), pos)
        s = jnp.where(hit, -jnp.inf, s)
    return top, pos


def _peer_select_exact_head(h, q_ref, keys_ref, cnt_ref, e1_ref, rank_ref, e2_ref):
    kk = PEER_TOPK
    dq = keys_ref.shape[3]
    hi = lax.Precision.HIGHEST
    col = pl.multiple_of(h * (2 * dq), 2 * dq)
    s1 = lax.dot_general(keys_ref[h, 0], q_ref[:, pl.ds(col, dq)], _NT, precision=hi,
                         preferred_element_type=_F32)
    s2 = lax.dot_general(keys_ref[h, 1], q_ref[:, pl.ds(col + dq, dq)], _NT, precision=hi,
                         preferred_element_type=_F32)
    a, pos1 = _topk_exact(s1, kk)
    b, pos2 = _topk_exact(s2, kk)
    cand = jnp.concatenate([a[r:r + 1, :] + b for r in range(kk)], axis=0)
    _, posc = _topk_exact(cand, kk)
    picked = posc < kk
    z = jnp.sum(jnp.where(picked, jnp.exp(cand - (a[0:1, :] + b[0:1, :])), 0.0),
                axis=0, keepdims=True)
    cnt1 = jnp.zeros(s1.shape, _F32)
    for r in range(kk):
        count = jnp.sum(jnp.where(picked[r * kk:(r + 1) * kk, :], 1.0, 0.0), axis=0, keepdims=True)
        cnt1 = jnp.where(pos1 == r, count, cnt1)
    cnt_ref[h] = _pair_bf16(cnt1)
    e1_ref[h] = _pair_bf16(jnp.exp(s1 - a[0:1, :]))
    rank_ref[h] = pos2.astype(_BF16)
    e2_ref[h] = (jnp.exp(s2 - b[0:1, :]) * (0.5 / z)).astype(_BF16)


def _count_at_least(groups, bound):
    n = None
    for g in groups:
        hit = jnp.where(g >= bound, 1.0, 0.0)
        n = hit if n is None else n + hit
    return jnp.sum(n, axis=0, keepdims=True)


def _peer_select_kernel(q_ref, keys_ref, cnt_ref, e1_ref, rank_ref, e2_ref):
    kk = PEER_TOPK
    sub = V7X_SUBLANES
    dq = keys_ref.shape[3]
    n_groups = keys_ref.shape[2] // sub
    assert kk == 16 and n_groups == 16
    hi = lax.Precision.HIGHEST
    tied = None
    for h in range(PEER_HEADS):
        q1 = q_ref[:, (2 * h) * dq:(2 * h + 1) * dq]
        q2 = q_ref[:, (2 * h + 1) * dq:(2 * h + 2) * dq]
        s1 = lax.dot_general(keys_ref[h, 0], q1, _NT, precision=hi,
                             preferred_element_type=_F32)
        s2 = lax.dot_general(keys_ref[h, 1], q2, _NT, precision=hi,
                             preferred_element_type=_F32)
        g1 = [s1[r * sub:(r + 1) * sub, :] for r in range(n_groups)]
        g2 = [s2[r * sub:(r + 1) * sub, :] for r in range(n_groups)]
        a = _top16_sorted(g1)
        b = _top16_sorted(g2)
        a16 = jnp.concatenate([_stack_rows(a[:sub]), _stack_rows(a[sub:])], axis=0)
        b16 = jnp.concatenate([_stack_rows(b[:sub]), _stack_rows(b[sub:])], axis=0)
        half = kk // 2
        cells = [a16[0:1, :] + b16[0:half, :], a16[0:1, :] + b16[half:kk, :]]
        cells += [a16[r:r + 1, :] + b16[0:half, :] for r in range(1, half)]
        cells += [a16[half:kk, :] + b16[0:1, :]]
        best = _top_values(jnp.concatenate(cells, axis=0), kk)
        tau = jnp.broadcast_to(best[kk - 1:kk, :], a[0].shape)
        over = (_count_at_least(g1, a[kk - 1]) + _count_at_least(g2, b[kk - 1])
                + _count_at_least(cells, tau)) != 3.0 * kk
        tied = over if tied is None else jnp.logical_or(tied, over)
        z = jnp.sum(jnp.exp(best - (a16[0:1, :] + b16[0:1, :])), axis=0, keepdims=True)
        inv_z = 0.5 / z
        cnt1, e1, rank2, e2 = [], [], [], []
        for r in range(n_groups):
            x1, x2 = g1[r], g2[r]
            cnt1.append(_prefix_count(b, lambda piv: x1 + piv >= tau))
            rank2.append(_prefix_count(b, lambda piv: piv > x2))
            e1.append(jnp.exp(x1 - a[0]))
            e2.append(jnp.exp(x2 - b[0]) * inv_z)
        cnt_ref[h] = _pair_bf16(jnp.concatenate(cnt1, axis=0))
        e1_ref[h] = _pair_bf16(jnp.concatenate(e1, axis=0))
        rank_ref[h] = jnp.concatenate(rank2, axis=0).astype(_BF16)
        e2_ref[h] = jnp.concatenate(e2, axis=0).astype(_BF16)

    @pl.when(jnp.max(jnp.where(tied, 1.0, 0.0)) > 0.0)
    def _():
        def redo(h, carry):
            _peer_select_exact_head(h, q_ref, keys_ref, cnt_ref, e1_ref, rank_ref, e2_ref)
            return carry

        lax.fori_loop(0, PEER_HEADS, redo, 0)


def _peer_select(q, sub_keys, *, tm):
    t = q.shape[0]
    nk = PEER_N_KEYS
    spec = pl.BlockSpec((PEER_HEADS, nk, tm), lambda i: (0, 0, i))
    pair = jax.ShapeDtypeStruct((PEER_HEADS, nk, t), jnp.uint32)
    half = jax.ShapeDtypeStruct((PEER_HEADS, nk, t), _BF16)
    return pl.pallas_call(
        _peer_select_kernel,
        grid=(t // tm,),
        in_specs=[pl.BlockSpec((tm, q.shape[1]), lambda i: (i, 0)),
                  pl.BlockSpec(sub_keys.shape, lambda i: (0, 0, 0, 0))],
        out_specs=[spec, spec, spec, spec],
        out_shape=[pair, pair, half, half],
        compiler_params=_params("parallel"),
        name="peer_select",
    )(q, sub_keys)


PEER_ROWS_PER_CHUNK = V7X_SUBLANES
PEER_LANE_CHUNK = 2 * V7X_LANES


def _bf16_rows(pair_row):
    tile = jnp.broadcast_to(pair_row, (V7X_SUBLANES, pair_row.shape[1]))
    return pltpu.bitcast(tile, _BF16)[0:V7X_SUBLANES]


def _peer_experts_kernel(xt_ref, u_ref, vt_ref, cnt_ref, e1_ref, rank_ref, e2_ref,
                         r_ref, lng_ref, lnb_ref, o_ref, hdn_ref, g_ref, acc_ref):
    e = pl.program_id(1)
    nk = PEER_N_KEYS
    tm = xt_ref.shape[1]
    sl = V7X_SUBLANES
    lw = PEER_LANE_CHUNK

    @pl.when(e == 0)
    def _():
        acc_ref[...] = jnp.zeros_like(acc_ref)

    hdn_ref[...] = jnp.dot(u_ref[...], xt_ref[...], preferred_element_type=_F32)
    sqrt_half = math.sqrt(0.5)
    for lc in range(tm // lw):
        lanes = slice(lc * lw, (lc + 1) * lw)
        for ii in range(PEER_ROWS_PER_CHUNK):
            w = None
            for h in range(PEER_HEADS):
                cnt = _bf16_rows(cnt_ref[h, ii:ii + 1, lanes])[None]
                e1 = _bf16_rows(e1_ref[h, ii:ii + 1, lanes])[None]
                rank = rank_ref[h, :, lanes].reshape(nk // sl, sl, lw)
                e2 = e2_ref[h, :, lanes].reshape(nk // sl, sl, lw)
                term = jnp.where(rank < cnt, e2, jnp.zeros_like(e2)) * e1
                w = term if w is None else w + term
            hh = hdn_ref[ii * nk:(ii + 1) * nk, lanes]
            act = hh * (1.0 + lax.erf(hh * sqrt_half))
            g_ref[ii * nk:(ii + 1) * nk, lanes] = w.reshape(nk, lw) * act.astype(_BF16)
    acc_ref[...] += jnp.dot(vt_ref[...], g_ref[...], preferred_element_type=_F32)

    @pl.when(e == pl.num_programs(1) - 1)
    def _():
        ff = acc_ref[...].T
        o_ref[...] = _layer_norm_rows(DEEPNORM_ALPHA * r_ref[...] + ff, lng_ref[...], lnb_ref[...])


def _peer_experts(xt, u, vt, cnt, e1, rank, e2, resid, ln_g, ln_b, *, tm):
    d, t = xt.shape
    n_exp = u.shape[0]
    nk = PEER_N_KEYS
    rows = PEER_ROWS_PER_CHUNK
    chunk = rows * nk
    return pl.pallas_call(
        _peer_experts_kernel,
        grid=(t // tm, n_exp // chunk),
        in_specs=[pl.BlockSpec((d, tm), lambda i, e: (0, i)),
                  pl.BlockSpec((chunk, d), lambda i, e: (e, 0)),
                  pl.BlockSpec((d, chunk), lambda i, e: (0, e)),
                  pl.BlockSpec((PEER_HEADS, rows, tm), lambda i, e: (0, e, i)),
                  pl.BlockSpec((PEER_HEADS, rows, tm), lambda i, e: (0, e, i)),
                  pl.BlockSpec((PEER_HEADS, nk, tm), lambda i, e: (0, 0, i)),
                  pl.BlockSpec((PEER_HEADS, nk, tm), lambda i, e: (0, 0, i)),
                  pl.BlockSpec((tm, d), lambda i, e: (i, 0)),
                  pl.BlockSpec((1, d), lambda i, e: (0, 0)),
                  pl.BlockSpec((1, d), lambda i, e: (0, 0))],
        out_specs=pl.BlockSpec((tm, d), lambda i, e: (i, 0)),
        out_shape=jax.ShapeDtypeStruct((t, d), _F32),
        scratch_shapes=[pltpu.VMEM((chunk, tm), _F32),
                        pltpu.VMEM((chunk, tm), _BF16),
                        pltpu.VMEM((d, tm), _F32)],
        compiler_params=_params("parallel", "arbitrary"),
        name="peer_experts",
    )(xt, u, vt, cnt, e1, rank, e2, resid, ln_g, ln_b)


def _layer(x, mem, w_in, gate_up, gate_bias, norm_g, w_out, ln1_g, ln1_b,
           wq, wkv, wo, ln2_g, ln2_b, peer_wq, sub_keys, peer_u, peer_v, ln3_g, ln3_b):
    batch, seq, d = x.shape
    mem_len = mem.shape[1]
    t = batch * seq
    x2 = x.reshape(t, d)

    moba_w = MOBA_HEADS * MOBA_HEAD_DIM
    gla_kw = GLA_HEADS * GLA_KEY_DIM
    gla_vw = GLA_HEADS * GLA_VAL_DIM
    cols = w_in.shape[1]
    cols_pad = -(-cols // _Blocks.proj_cols) * _Blocks.proj_cols
    w_in_p = jnp.pad(w_in, ((0, 0), (0, cols_pad - cols))).astype(_BF16)
    proj = _matmul(x2, w_in_p, bm=_Blocks.matmul_rows, bn=_Blocks.proj_cols, out_dtype=_F32)

    q_m, k_m, v_m = 0, moba_w, 2 * moba_w
    q_g = 3 * moba_w
    k_g = q_g + gla_kw
    v_g = k_g + gla_kw
    r_g = v_g + gla_vw
    lr = r_g + gla_vw
    o_moba = _moba(proj, batch=batch, seq=seq, q_col=q_m // MOBA_HEAD_DIM,
                   k_col=k_m // MOBA_HEAD_DIM, v_col=v_m // MOBA_HEAD_DIM)
    gate_up_p = jnp.pad(gate_up, ((0, V7X_LANES - gate_up.shape[0]), (0, 0)))
    o_gla = _gla(proj, gate_up_p, gate_bias.reshape(1, -1), norm_g.reshape(1, -1),
                 batch=batch, seq=seq, q_col=q_g // GLA_KEY_DIM, k_col=k_g // GLA_KEY_DIM,
                 v_col=v_g // GLA_VAL_DIM, r_col=r_g // GLA_VAL_DIM, lr_col=lr // V7X_LANES)
    h1, h1_bf = _matmul_residual_ln(o_moba, 0, o_gla, 0, w_out.astype(_BF16), x2,
                                    ln1_g.reshape(1, -1), ln1_b.reshape(1, -1),
                                    bm=_Blocks.ln_rows)

    dense = dict(bm=_Blocks.matmul_rows, bn=_Blocks.matmul_cols)
    q_x = _matmul(h1_bf, wq.astype(_BF16), out_dtype=_BF16, **dense)
    kv = _matmul(mem.reshape(batch * mem_len, d), wkv.astype(_BF16), out_dtype=_BF16, **dense)
    o_x = _xattn(q_x, kv, batch=batch, seq=seq, mem_len=mem_len, bq=_Blocks.xattn_rows)
    h2, h2_bf, h2_bf_t = _matmul_residual_ln(
        o_x, 0, o_x, 1, wo.astype(_BF16), h1, ln2_g.reshape(1, -1), ln2_b.reshape(1, -1),
        bm=_Blocks.ln_rows, transposed_copy=True)

    q_p = _matmul(h2_bf, peer_wq.astype(_BF16), out_dtype=_F32, **dense)
    cnt, e1, rank, e2 = _peer_select(q_p, sub_keys, tm=_Blocks.peer_select_tokens)
    y = _peer_experts(h2_bf_t, peer_u.astype(_BF16), peer_v.astype(_BF16).T,
                      cnt, e1, rank, e2, h2, ln3_g.reshape(1, -1), ln3_b.reshape(1, -1),
                      tm=_Blocks.peer_expert_tokens)
    return y.reshape(batch, seq, d)


def kernel(x, mem, w_in, gla_gate_up, gla_gate_bias, gla_norm_g, w_out, ln1_g, ln1_b,
           xattn_wq, xattn_wkv, xattn_wo, ln2_g, ln2_b,
           peer_wq, peer_sub_keys, peer_u, peer_v, ln3_g, ln3_b):
    h = x
    for l in range(w_in.shape[0]):
        h = _layer(h, mem, w_in[l], gla_gate_up[l], gla_gate_bias[l], gla_norm_g[l], w_out[l],
                   ln1_g[l], ln1_b[l], xattn_wq[l], xattn_wkv[l], xattn_wo[l], ln2_g[l], ln2_b[l],
                   peer_wq[l], peer_sub_keys[l], peer_u[l], peer_v[l], ln3_g[l], ln3_b[l])
    return h
```

```python
import functools
import math

import jax
import jax.numpy as jnp
import numpy as np
from jax import lax
from jax.experimental import pallas as pl
from jax.experimental.pallas import tpu as pltpu

MOBA_HEADS = 8
MOBA_HEAD_DIM = 128
MOBA_BLOCK = 256
MOBA_TOPK = 3
GLA_HEADS = 4
GLA_KEY_DIM = 128
GLA_VAL_DIM = 256
GLA_GATE_RANK = 16
GLA_GATE_TAU = 16.0
GLA_CHUNK = 64
XATTN_HEADS = 4
PEER_HEADS = 8
PEER_N_KEYS = 128
PEER_TOPK = 16
DEPTH = 1
DEEPNORM_ALPHA = (2.0 * DEPTH) ** 0.25
LN_EPS = 1e-5
RMS_EPS = 1e-6
NEG_INF = -1e30

V7X_LANES = 128
V7X_SUBLANES = 8
V7X_VMEM_LIMIT_BYTES = 56 * 1024 * 1024
V7X_MXU_COLS = 256


class _Blocks:
    matmul_rows = 1024
    matmul_cols = 4 * V7X_MXU_COLS
    proj_cols = 5 * V7X_MXU_COLS
    ln_rows = 512
    xattn_rows = 512
    peer_select_tokens = 2 * V7X_LANES
    peer_expert_tokens = 4 * V7X_LANES

_BF16 = jnp.bfloat16
_F32 = jnp.float32
_NT = (((1,), (1,)), ((), ()))
_TN = (((0,), (0,)), ((), ()))


def _params(*semantics):
    return pltpu.CompilerParams(dimension_semantics=semantics,
                                vmem_limit_bytes=V7X_VMEM_LIMIT_BYTES)


def _matmul_kernel(a_ref, b_ref, o_ref, *maybe_abf_ref):
    if maybe_abf_ref:
        abf_ref, = maybe_abf_ref

        @pl.when(pl.program_id(1) == 0)
        def _():
            abf_ref[...] = a_ref[...].astype(_BF16)

        a = abf_ref[...]
    else:
        a = a_ref[...]
    o_ref[...] = jnp.dot(a, b_ref[...], preferred_element_type=_F32).astype(o_ref.dtype)


def _matmul(a, b, *, bm, bn, out_dtype):
    m, k = a.shape
    _, n = b.shape
    bm = min(bm, m)
    assert m % bm == 0 and n % bn == 0
    cast_in_kernel = a.dtype != _BF16
    return pl.pallas_call(
        _matmul_kernel,
        grid=(m // bm, n // bn),
        in_specs=[pl.BlockSpec((bm, k), lambda i, j: (i, 0)),
                  pl.BlockSpec((k, bn), lambda i, j: (0, j))],
        out_specs=pl.BlockSpec((bm, bn), lambda i, j: (i, j)),
        out_shape=jax.ShapeDtypeStruct((m, n), out_dtype),
        scratch_shapes=[pltpu.VMEM((bm, k), _BF16)] if cast_in_kernel else [],
        compiler_params=_params("parallel", "arbitrary" if cast_in_kernel else "parallel"),
        name="matmul",
    )(a, b)


def _layer_norm_rows(z, g, b):
    mu = jnp.mean(z, axis=-1, keepdims=True)
    zc = z - mu
    var = jnp.mean(zc * zc, axis=-1, keepdims=True)
    return zc * lax.rsqrt(var + LN_EPS) * g + b


def _matmul_residual_ln_kernel(a1_ref, a2_ref, w_ref, r_ref, g_ref, b_ref, o_ref, obf_ref,
                               *maybe_obt_ref):
    kh = a1_ref.shape[1]
    acc = jnp.dot(a1_ref[...], w_ref[:kh, :], preferred_element_type=_F32)
    acc = acc + jnp.dot(a2_ref[...], w_ref[kh:, :], preferred_element_type=_F32)
    y = _layer_norm_rows(DEEPNORM_ALPHA * r_ref[...] + acc, g_ref[...], b_ref[...])
    o_ref[...] = y
    obf_ref[...] = y.astype(_BF16)
    for obt_ref in maybe_obt_ref:
        obt_ref[...] = y.T.astype(_BF16)


def _matmul_residual_ln(a1, a1_col, a2, a2_col, w, resid, g, b, *, bm, transposed_copy=False):
    m, d = resid.shape
    k = w.shape[0]
    kh = k // 2
    out_specs = [pl.BlockSpec((bm, d), lambda i: (i, 0)), pl.BlockSpec((bm, d), lambda i: (i, 0))]
    out_shape = [jax.ShapeDtypeStruct((m, d), _F32), jax.ShapeDtypeStruct((m, d), _BF16)]
    if transposed_copy:
        out_specs.append(pl.BlockSpec((d, bm), lambda i: (0, i)))
        out_shape.append(jax.ShapeDtypeStruct((d, m), _BF16))
    return pl.pallas_call(
        _matmul_residual_ln_kernel,
        grid=(m // bm,),
        in_specs=[pl.BlockSpec((bm, kh), lambda i: (i, a1_col)),
                  pl.BlockSpec((bm, kh), lambda i: (i, a2_col)),
                  pl.BlockSpec((k, d), lambda i: (0, 0)),
                  pl.BlockSpec((bm, d), lambda i: (i, 0)),
                  pl.BlockSpec((1, d), lambda i: (0, 0)),
                  pl.BlockSpec((1, d), lambda i: (0, 0))],
        out_specs=out_specs,
        out_shape=out_shape,
        compiler_params=_params("parallel"),
        name="matmul_residual_ln",
    )(a1, a2, w, resid, g, b)


def _moba_select_blocks(q_ref, k_ref, sel_ref, n_blocks):
    bs = MOBA_BLOCK
    seq, dh = q_ref.shape
    kmean = jnp.sum(k_ref[...].reshape(n_blocks, bs, dh), axis=1) * (1.0 / bs)
    gate = lax.dot_general(kmean, q_ref[...], _NT, precision=lax.Precision.HIGHEST,
                           preferred_element_type=_F32)
    blk = lax.broadcasted_iota(jnp.int32, gate.shape, 0)
    tok = lax.broadcasted_iota(jnp.int32, gate.shape, 1)
    gate = jnp.where(tok >= (blk + 1) * bs, gate, NEG_INF)
    rank = jnp.zeros(gate.shape, _F32)
    for j in range(n_blocks):
        gj = gate[j:j + 1, :]
        ahead = jnp.logical_or(gj > gate, jnp.logical_and(gj == gate, blk > j))
        rank = rank + jnp.where(ahead, 1.0, 0.0)
    picked = jnp.logical_and(rank < MOBA_TOPK, gate > 0.5 * NEG_INF)
    sel_t = jnp.where(picked, 1.0, 0.0)
    sel_t = jnp.concatenate([sel_t, jnp.zeros((V7X_LANES - n_blocks, seq), _F32)], axis=0)
    sel_ref[...] = sel_t.T


def _moba_block_body(q_ref, kb_ref, vb_ref, o_ref, sel_ref, n_past):
    bs = MOBA_BLOCK
    dh = q_ref.shape[1]
    n_keys = (n_past + 1) * bs
    own = slice(n_past * bs, n_keys)
    qb = (q_ref[own, :] * (dh ** -0.5)).astype(_BF16)
    row = lax.broadcasted_iota(jnp.int32, (bs, bs), 0)
    col = lax.broadcasted_iota(jnp.int32, (bs, bs), 1)
    parts = []
    if n_past > 0:
        sel = sel_ref[own, :]
        for j in range(n_past):
            s = lax.dot_general(qb, kb_ref[j * bs:(j + 1) * bs, :], _NT,
                                preferred_element_type=_F32)
            parts.append(jnp.where(sel[:, j:j + 1] > 0.0, s, NEG_INF))
    s = lax.dot_general(qb, kb_ref[own, :], _NT, preferred_element_type=_F32)
    parts.append(jnp.where(col <= row, s, NEG_INF))
    s_all = jnp.concatenate(parts, axis=-1)
    m = jnp.max(s_all, axis=-1, keepdims=True)
    p = jnp.exp(s_all - m)
    l = jnp.sum(p, axis=-1, keepdims=True)
    o = jnp.dot(p.astype(_BF16), vb_ref[0:n_keys, :], preferred_element_type=_F32)
    o_ref[own, :] = (o / l).astype(o_ref.dtype)


def _moba_kernel(q_ref, k_ref, v_ref, o_ref, sel_ref, kb_ref, vb_ref, *, n_blocks):
    if n_blocks > 1:
        _moba_select_blocks(q_ref, k_ref, sel_ref, n_blocks)
    kb_ref[...] = k_ref[...].astype(_BF16)
    vb_ref[...] = v_ref[...].astype(_BF16)
    for c in range(n_blocks):
        _moba_block_body(q_ref, kb_ref, vb_ref, o_ref, sel_ref, c)


def _moba(proj, *, batch, seq, q_col, k_col, v_col):
    nb = seq // MOBA_BLOCK
    dh = MOBA_HEAD_DIM
    return pl.pallas_call(
        functools.partial(_moba_kernel, n_blocks=nb),
        grid=(batch, MOBA_HEADS),
        in_specs=[pl.BlockSpec((seq, dh), lambda b, h: (b, q_col + h)),
                  pl.BlockSpec((seq, dh), lambda b, h: (b, k_col + h)),
                  pl.BlockSpec((seq, dh), lambda b, h: (b, v_col + h))],
        out_specs=pl.BlockSpec((seq, dh), lambda b, h: (b, h)),
        out_shape=jax.ShapeDtypeStruct((batch * seq, MOBA_HEADS * dh), _BF16),
        scratch_shapes=[pltpu.VMEM((seq, V7X_LANES), _F32),
                        pltpu.VMEM((seq, dh), _BF16),
                        pltpu.VMEM((seq, dh), _BF16)],
        compiler_params=_params("parallel", "parallel"),
        name="moba",
    )(proj, proj, proj)


GLA_GROUP = 4
GLA_UNROLL = 4


def _log_sigmoid(z):
    return jnp.minimum(z, 0.0) - jnp.log(1.0 + jnp.exp(-jnp.abs(z)))


def _gla_kernel(q_ref, k_ref, v_ref, r_ref, lr_ref, up_ref, bias_ref, g_ref, o_ref,
                la_ref, qd_ref, ks_ref, oi_ref, dec_ref, *, n_chunks):
    c = GLA_CHUNK
    dk = q_ref.shape[1]
    dv = v_ref.shape[1]
    sub = V7X_SUBLANES
    gain = g_ref[...]
    hi = lax.Precision.HIGHEST

    z = jnp.dot(lr_ref[...], up_ref[...], precision=hi, preferred_element_type=_F32)
    la_ref[...] = _log_sigmoid(z + bias_ref[...]) * (1.0 / GLA_GATE_TAU)

    grp = GLA_GROUP
    gr = grp * c
    row = lax.broadcasted_iota(jnp.int32, (gr, gr), 0)
    col = lax.broadcasted_iota(jnp.int32, (gr, gr), 1)
    shift = c.bit_length() - 1
    causal = jnp.logical_and(jnp.right_shift(row, shift) == jnp.right_shift(col, shift), col <= row)
    causal_f = causal.astype(_F32)
    for g in range(n_chunks // grp):
        rows = slice(g * gr, (g + 1) * gr)
        cum = jnp.dot(causal_f, la_ref[rows, :], precision=hi, preferred_element_type=_F32)
        cum_last = jnp.concatenate(
            [jnp.broadcast_to(cum[(u + 1) * c - 1:(u + 1) * c, :], (c, dk)) for u in range(grp)],
            axis=0)
        q = q_ref[rows, :] * (dk ** -0.5)
        k = k_ref[rows, :]
        q_dec = (q * jnp.exp(cum)).astype(_BF16)
        k_dec = (k * jnp.exp(-cum)).astype(_BF16)
        a = lax.dot_general(q_dec, k_dec, _NT, preferred_element_type=_F32)
        a = jnp.where(causal, a, 0.0).astype(_BF16)
        qd_ref[rows, :] = q_dec
        ks_ref[rows, :] = (k * jnp.exp(cum_last - cum)).astype(_BF16)
        oi_ref[rows, :] = jnp.dot(a, v_ref[rows, :].astype(_BF16), preferred_element_type=_F32)
        for u in range(grp):
            n = g * grp + u
            dec_ref[n * sub:(n + 1) * sub, :] = jnp.exp(cum_last[u * c:u * c + sub, :])

    def inter(n, state_t):
        rows = pl.ds(pl.multiple_of(n * c, c), c)
        o = oi_ref[rows, :] + lax.dot_general(qd_ref[rows, :], state_t.astype(_BF16), _NT,
                                              preferred_element_type=_F32)
        delta_t = lax.dot_general(v_ref[rows, :].astype(_BF16), ks_ref[rows, :], _TN,
                                  preferred_element_type=_F32)
        decay = dec_ref[pl.ds(pl.multiple_of(n * sub, sub), 1), :]
        o = o * lax.rsqrt(jnp.mean(o * o, axis=-1, keepdims=True) + RMS_EPS) * gain
        r = r_ref[rows, :]
        o_ref[rows, :] = (o * (r * jax.nn.sigmoid(r))).astype(o_ref.dtype)
        return state_t * decay + delta_t

    lax.fori_loop(0, n_chunks, inter, jnp.zeros((dv, dk), _F32), unroll=GLA_UNROLL)


def _gla(proj, gate_up, gate_bias, norm_g, *, batch, seq, q_col, k_col, v_col, r_col, lr_col):
    dk, dv = GLA_KEY_DIM, GLA_VAL_DIM
    return pl.pallas_call(
        functools.partial(_gla_kernel, n_chunks=seq // GLA_CHUNK),
        grid=(batch, GLA_HEADS),
        in_specs=[pl.BlockSpec((seq, dk), lambda b, h: (b, q_col + h)),
                  pl.BlockSpec((seq, dk), lambda b, h: (b, k_col + h)),
                  pl.BlockSpec((seq, dv), lambda b, h: (b, v_col + h)),
                  pl.BlockSpec((seq, dv), lambda b, h: (b, r_col + h)),
                  pl.BlockSpec((seq, V7X_LANES), lambda b, h: (b, lr_col)),
                  pl.BlockSpec((V7X_LANES, dk), lambda b, h: (0, h)),
                  pl.BlockSpec((1, dk), lambda b, h: (0, h)),
                  pl.BlockSpec((1, dv), lambda b, h: (0, 0))],
        out_specs=pl.BlockSpec((seq, dv), lambda b, h: (b, h)),
        out_shape=jax.ShapeDtypeStruct((batch * seq, GLA_HEADS * dv), _BF16),
        scratch_shapes=[pltpu.VMEM((seq, dk), _F32),
                        pltpu.VMEM((seq, dk), _BF16),
                        pltpu.VMEM((seq, dk), _BF16),
                        pltpu.VMEM((seq, dv), _F32),
                        pltpu.VMEM((seq // GLA_CHUNK * V7X_SUBLANES, dk), _F32)],
        compiler_params=_params("parallel", "parallel"),
        name="gla",
    )(proj, proj, proj, proj, proj, gate_up, gate_bias, norm_g)


def _xattn_kernel(q_ref, kv_ref, o_ref):
    d = q_ref.shape[1]
    hd = d // XATTN_HEADS
    scale = hd ** -0.5
    for h in range(XATTN_HEADS):
        qh = q_ref[:, h * hd:(h + 1) * hd]
        kh = kv_ref[:, h * hd:(h + 1) * hd]
        vh = kv_ref[:, d + h * hd:d + (h + 1) * hd]
        s = lax.dot_general(qh, kh, _NT, preferred_element_type=_F32) * scale
        m = jnp.max(s, axis=-1, keepdims=True)
        p = jnp.exp(s - m)
        l = jnp.sum(p, axis=-1, keepdims=True)
        o = jnp.dot(p.astype(_BF16), vh, preferred_element_type=_F32)
        o_ref[:, h * hd:(h + 1) * hd] = (o / l).astype(o_ref.dtype)


def _xattn(q, kv, *, batch, seq, mem_len, bq):
    d = q.shape[1]
    nq = seq // bq
    return pl.pallas_call(
        _xattn_kernel,
        grid=(batch, nq),
        in_specs=[pl.BlockSpec((bq, d), lambda b, i: (b * nq + i, 0)),
                  pl.BlockSpec((mem_len, 2 * d), lambda b, i: (b, 0))],
        out_specs=pl.BlockSpec((bq, d), lambda b, i: (b * nq + i, 0)),
        out_shape=jax.ShapeDtypeStruct((batch * seq, d), _BF16),
        compiler_params=_params("parallel", "parallel"),
        name="xattn",
    )(q, kv)


def _sorting_network(n):
    pairs = []
    p = 1
    while p < n:
        k = p
        while k >= 1:
            for j in range(k % p, n - k, 2 * k):
                for i in range(min(k, n - j - k)):
                    if (i + j) // (2 * p) == (i + j + k) // (2 * p):
                        pairs.append((i + j, i + j + k))
            k //= 2
        p *= 2
    return pairs


def _compare_exchange(c, i, j):
    c[i], c[j] = jnp.maximum(c[i], c[j]), jnp.minimum(c[i], c[j])


def _top16_sorted(groups):
    k = len(groups)
    c = list(groups)
    for i, j in _sorting_network(k):
        _compare_exchange(c, i, j)
    shift = V7X_SUBLANES // 2
    while shift >= 1:
        p = [pltpu.roll(x, shift, axis=0) for x in c]
        c = [jnp.maximum(c[i], p[k - 1 - i]) for i in range(k)]
        d = k // 2
        while d >= 1:
            for i in range(k):
                if not i & d:
                    _compare_exchange(c, i, i + d)
            d //= 2
        shift //= 2
    return c


def _prefix_count(vals, pred):
    g1 = pred(vals[7])
    g2 = pred(jnp.where(g1, vals[11], vals[3]))
    g3 = pred(jnp.where(g1, jnp.where(g2, vals[13], vals[9]), jnp.where(g2, vals[5], vals[1])))
    lo = jnp.where(g2, jnp.where(g3, vals[6], vals[4]), jnp.where(g3, vals[2], vals[0]))
    hi = jnp.where(g2, jnp.where(g3, vals[14], vals[12]), jnp.where(g3, vals[10], vals[8]))
    g4 = pred(jnp.where(g1, hi, lo))
    g5 = pred(vals[15])
    return (jnp.where(g1, 8.0, 0.0) + jnp.where(g2, 4.0, 0.0) + jnp.where(g3, 2.0, 0.0)
            + jnp.where(g4, 1.0, 0.0) + jnp.where(g5, 1.0, 0.0))


def _stack_rows(reps):
    sub = lax.broadcasted_iota(jnp.int32, reps[0].shape, 0)
    out = reps[0]
    for d in range(1, len(reps)):
        out = jnp.where(sub == d, reps[d], out)
    return out


def _pair_bf16(x):
    u = pltpu.bitcast(x.astype(_BF16).astype(_F32), jnp.uint32)
    return u | lax.shift_right_logical(u, jnp.uint32(16))


def _topk_exact(s, k):
    rows, tm = s.shape
    row = lax.broadcasted_iota(jnp.int32, s.shape, 0).astype(_F32)
    rid = lax.broadcasted_iota(jnp.int32, (k, tm), 0)
    top = jnp.zeros((k, tm), _F32)
    pos = jnp.full(s.shape, float(k), _F32)
    for r in range(k):
        m = jnp.max(s, axis=0, keepdims=True)
        first = jnp.min(jnp.where(s == m, row, float(rows)), axis=0, keepdims=True)
        hit = row == first
        top = jnp.where(rid == r, m, top)
        pos = jnp.where(hit, float(r), pos)
        s = jnp.where(hit, -jnp.inf, s)
    return top, pos


def _peer_select_exact_head(h, q_ref, keys_ref, cnt_ref, e1_ref, rank_ref, e2_ref):
    kk = PEER_TOPK
    dq = keys_ref.shape[3]
    hi = lax.Precision.HIGHEST
    col = pl.multiple_of(h * (2 * dq), 2 * dq)
    s1 = lax.dot_general(keys_ref[h, 0], q_ref[:, pl.ds(col, dq)], _NT, precision=hi,
                         preferred_element_type=_F32)
    s2 = lax.dot_general(keys_ref[h, 1], q_ref[:, pl.ds(col + dq, dq)], _NT, precision=hi,
                         preferred_element_type=_F32)
    a, pos1 = _topk_exact(s1, kk)
    b, pos2 = _topk_exact(s2, kk)
    cand = jnp.concatenate([a[r:r + 1, :] + b for r in range(kk)], axis=0)
    _, posc = _topk_exact(cand, kk)
    picked = posc < kk
    z = jnp.sum(jnp.where(picked, jnp.exp(cand - (a[0:1, :] + b[0:1, :])), 0.0),
                axis=0, keepdims=True)
    cnt1 = jnp.zeros(s1.shape, _F32)
    for r in range(kk):
        count = jnp.sum(jnp.where(picked[r * kk:(r + 1) * kk, :], 1.0, 0.0), axis=0, keepdims=True)
        cnt1 = jnp.where(pos1 == r, count, cnt1)
    cnt_ref[h] = _pair_bf16(cnt1)
    e1_ref[h] = _pair_bf16(jnp.exp(s1 - a[0:1, :]))
    rank_ref[h] = pos2.astype(_BF16)
    e2_ref[h] = (jnp.exp(s2 - b[0:1, :]) * (0.5 / z)).astype(_BF16)


def _count_at_least(groups, bound):
    n = None
    for g in groups:
        hit = jnp.where(g >= bound, 1.0, 0.0)
        n = hit if n is None else n + hit
    return jnp.sum(n, axis=0, keepdims=True)


def _peer_select_kernel(q_ref, keys_ref, cnt_ref, e1_ref, rank_ref, e2_ref, tied_ref):
    kk = PEER_TOPK
    sub = V7X_SUBLANES
    dq = keys_ref.shape[3]
    n_groups = keys_ref.shape[2] // sub
    assert kk == 16 and n_groups == 16
    hi = lax.Precision.HIGHEST
    for h in range(PEER_HEADS):
        q1 = q_ref[:, (2 * h) * dq:(2 * h + 1) * dq]
        q2 = q_ref[:, (2 * h + 1) * dq:(2 * h + 2) * dq]
        s1 = lax.dot_general(keys_ref[h, 0], q1, _NT, precision=hi,
                             preferred_element_type=_F32)
        s2 = lax.dot_general(keys_ref[h, 1], q2, _NT, precision=hi,
                             preferred_element_type=_F32)
        g1 = [s1[r * sub:(r + 1) * sub, :] for r in range(n_groups)]
        g2 = [s2[r * sub:(r + 1) * sub, :] for r in range(n_groups)]
        a = _top16_sorted(g1)
        b = _top16_sorted(g2)
        a16 = jnp.concatenate([_stack_rows(a[:sub]), _stack_rows(a[sub:])], axis=0)
        b16 = jnp.concatenate([_stack_rows(b[:sub]), _stack_rows(b[sub:])], axis=0)
        half = kk // 2
        cells = [a16[0:1, :] + b16[0:half, :], a16[0:1, :] + b16[half:kk, :]]
        cells += [a16[r:r + 1, :] + b16[0:half, :] for r in range(1, half)]
        cells += [a16[half:kk, :] + b16[0:1, :]]
        lowest = cells[0] + NEG_INF
        best = _top16_sorted(cells + [lowest] * (kk - len(cells)))
        tau = best[kk - 1]
        over = (_count_at_least(g1, a[kk - 1]) + _count_at_least(g2, b[kk - 1])
                + _count_at_least(cells, tau)) != 3.0 * kk
        tied_ref[h:h + 1, :] = jnp.where(over, 1.0, 0.0)
        best16 = jnp.concatenate([_stack_rows(best[:sub]), _stack_rows(best[sub:])], axis=0)
        z = jnp.sum(jnp.exp(best16 - (a16[0:1, :] + b16[0:1, :])), axis=0, keepdims=True)
        inv_z = 0.5 / z
        cnt1, e1, rank2, e2 = [], [], [], []
        for r in range(n_groups):
            x1, x2 = g1[r], g2[r]
            cnt1.append(_prefix_count(b, lambda piv: x1 + piv >= tau))
            rank2.append(_prefix_count(b, lambda piv: piv > x2))
            e1.append(jnp.exp(x1 - a[0]))
            e2.append(jnp.exp(x2 - b[0]) * inv_z)
        cnt_ref[h] = _pair_bf16(jnp.concatenate(cnt1, axis=0))
        e1_ref[h] = _pair_bf16(jnp.concatenate(e1, axis=0))
        rank_ref[h] = jnp.concatenate(rank2, axis=0).astype(_BF16)
        e2_ref[h] = jnp.concatenate(e2, axis=0).astype(_BF16)

    @pl.when(jnp.max(tied_ref[...]) > 0.0)
    def _():
        def redo(h, carry):
            _peer_select_exact_head(h, q_ref, keys_ref, cnt_ref, e1_ref, rank_ref, e2_ref)
            return carry

        lax.fori_loop(0, PEER_HEADS, redo, 0)


def _peer_select(q, sub_keys, *, tm):
    t = q.shape[0]
    nk = PEER_N_KEYS
    spec = pl.BlockSpec((PEER_HEADS, nk, tm), lambda i: (0, 0, i))
    pair = jax.ShapeDtypeStruct((PEER_HEADS, nk, t), jnp.uint32)
    half = jax.ShapeDtypeStruct((PEER_HEADS, nk, t), _BF16)
    return pl.pallas_call(
        _peer_select_kernel,
        grid=(t // tm,),
        in_specs=[pl.BlockSpec((tm, q.shape[1]), lambda i: (i, 0)),
                  pl.BlockSpec(sub_keys.shape, lambda i: (0, 0, 0, 0))],
        out_specs=[spec, spec, spec, spec],
        out_shape=[pair, pair, half, half],
        scratch_shapes=[pltpu.VMEM((PEER_HEADS, tm), _F32)],
        compiler_params=_params("parallel"),
        name="peer_select",
    )(q, sub_keys)


PEER_ROWS_PER_CHUNK = V7X_SUBLANES
PEER_LANE_CHUNK = 2 * V7X_LANES


def _bf16_rows(pair_row):
    tile = jnp.broadcast_to(pair_row, (V7X_SUBLANES, pair_row.shape[1]))
    return pltpu.bitcast(tile, _BF16)[0:V7X_SUBLANES]


def _peer_experts_kernel(xt_ref, u_ref, vt_ref, cnt_ref, e1_ref, rank_ref, e2_ref,
                         r_ref, lng_ref, lnb_ref, o_ref, hdn_ref, g_ref, acc_ref):
    e = pl.program_id(1)
    nk = PEER_N_KEYS
    tm = xt_ref.shape[1]
    sl = V7X_SUBLANES
    lw = PEER_LANE_CHUNK

    @pl.when(e == 0)
    def _():
        acc_ref[...] = jnp.zeros_like(acc_ref)

    hdn_ref[...] = jnp.dot(u_ref[...], xt_ref[...], preferred_element_type=_F32)
    sqrt_half = math.sqrt(0.5)
    for lc in range(tm // lw):
        lanes = slice(lc * lw, (lc + 1) * lw)
        for ii in range(PEER_ROWS_PER_CHUNK):
            w = None
            for h in range(PEER_HEADS):
                cnt = _bf16_rows(cnt_ref[h, ii:ii + 1, lanes])[None]
                e1 = _bf16_rows(e1_ref[h, ii:ii + 1, lanes])[None]
                rank = rank_ref[h, :, lanes].reshape(nk // sl, sl, lw)
                e2 = e2_ref[h, :, lanes].reshape(nk // sl, sl, lw)
                term = jnp.where(rank < cnt, e2, jnp.zeros_like(e2)) * e1
                w = term if w is None else w + term
            hh = hdn_ref[ii * nk:(ii + 1) * nk, lanes]
            act = hh * (1.0 + lax.erf(hh * sqrt_half))
            g_ref[ii * nk:(ii + 1) * nk, lanes] = w.reshape(nk, lw) * act.astype(_BF16)
    acc_ref[...] += jnp.dot(vt_ref[...], g_ref[...], preferred_element_type=_F32)

    @pl.when(e == pl.num_programs(1) - 1)
    def _():
        ff = acc_ref[...].T
        o_ref[...] = _layer_norm_rows(DEEPNORM_ALPHA * r_ref[...] + ff, lng_ref[...], lnb_ref[...])


def _peer_experts(xt, u, vt, cnt, e1, rank, e2, resid, ln_g, ln_b, *, tm):
    d, t = xt.shape
    n_exp = u.shape[0]
    nk = PEER_N_KEYS
    rows = PEER_ROWS_PER_CHUNK
    chunk = rows * nk
    return pl.pallas_call(
        _peer_experts_kernel,
        grid=(t // tm, n_exp // chunk),
        in_specs=[pl.BlockSpec((d, tm), lambda i, e: (0, i)),
                  pl.BlockSpec((chunk, d), lambda i, e: (e, 0)),
                  pl.BlockSpec((d, chunk), lambda i, e: (0, e)),
                  pl.BlockSpec((PEER_HEADS, rows, tm), lambda i, e: (0, e, i)),
                  pl.BlockSpec((PEER_HEADS, rows, tm), lambda i, e: (0, e, i)),
                  pl.BlockSpec((PEER_HEADS, nk, tm), lambda i, e: (0, 0, i)),
                  pl.BlockSpec((PEER_HEADS, nk, tm), lambda i, e: (0, 0, i)),
                  pl.BlockSpec((tm, d), lambda i, e: (i, 0)),
                  pl.BlockSpec((1, d), lambda i, e: (0, 0)),
                  pl.BlockSpec((1, d), lambda i, e: (0, 0))],
        out_specs=pl.BlockSpec((tm, d), lambda i, e: (i, 0)),
        out_shape=jax.ShapeDtypeStruct((t, d), _F32),
        scratch_shapes=[pltpu.VMEM((chunk, tm), _F32),
                        pltpu.VMEM((chunk, tm), _BF16),
                        pltpu.VMEM((d, tm), _F32)],
        compiler_params=_params("parallel", "arbitrary"),
        name="peer_experts",
    )(xt, u, vt, cnt, e1, rank, e2, resid, ln_g, ln_b)


def _layer(x, mem, w_in, gate_up, gate_bias, norm_g, w_out, ln1_g, ln1_b,
           wq, wkv, wo, ln2_g, ln2_b, peer_wq, sub_keys, peer_u, peer_v, ln3_g, ln3_b):
    batch, seq, d = x.shape
    mem_len = mem.shape[1]
    t = batch * seq
    x2 = x.reshape(t, d)

    moba_w = MOBA_HEADS * MOBA_HEAD_DIM
    gla_kw = GLA_HEADS * GLA_KEY_DIM
    gla_vw = GLA_HEADS * GLA_VAL_DIM
    cols = w_in.shape[1]
    cols_pad = -(-cols // _Blocks.proj_cols) * _Blocks.proj_cols
    w_in_p = jnp.pad(w_in, ((0, 0), (0, cols_pad - cols))).astype(_BF16)
    proj = _matmul(x2, w_in_p, bm=_Blocks.matmul_rows, bn=_Blocks.proj_cols, out_dtype=_F32)

    q_m, k_m, v_m = 0, moba_w, 2 * moba_w
    q_g = 3 * moba_w
    k_g = q_g + gla_kw
    v_g = k_g + gla_kw
    r_g = v_g + gla_vw
    lr = r_g + gla_vw
    o_moba = _moba(proj, batch=batch, seq=seq, q_col=q_m // MOBA_HEAD_DIM,
                   k_col=k_m // MOBA_HEAD_DIM, v_col=v_m // MOBA_HEAD_DIM)
    gate_up_p = jnp.pad(gate_up, ((0, V7X_LANES - gate_up.shape[0]), (0, 0)))
    o_gla = _gla(proj, gate_up_p, gate_bias.reshape(1, -1), norm_g.reshape(1, -1),
                 batch=batch, seq=seq, q_col=q_g // GLA_KEY_DIM, k_col=k_g // GLA_KEY_DIM,
                 v_col=v_g // GLA_VAL_DIM, r_col=r_g // GLA_VAL_DIM, lr_col=lr // V7X_LANES)
    h1, h1_bf = _matmul_residual_ln(o_moba, 0, o_gla, 0, w_out.astype(_BF16), x2,
                                    ln1_g.reshape(1, -1), ln1_b.reshape(1, -1),
                                    bm=_Blocks.ln_rows)

    dense = dict(bm=_Blocks.matmul_rows, bn=_Blocks.matmul_cols)
    q_x = _matmul(h1_bf, wq.astype(_BF16), out_dtype=_BF16, **dense)
    kv = _matmul(mem.reshape(batch * mem_len, d), wkv.astype(_BF16), out_dtype=_BF16, **dense)
    o_x = _xattn(q_x, kv, batch=batch, seq=seq, mem_len=mem_len, bq=_Blocks.xattn_rows)
    h2, h2_bf, h2_bf_t = _matmul_residual_ln(
        o_x, 0, o_x, 1, wo.astype(_BF16), h1, ln2_g.reshape(1, -1), ln2_b.reshape(1, -1),
        bm=_Blocks.ln_rows, transposed_copy=True)

    q_p = _matmul(h2_bf, peer_wq.astype(_BF16), out_dtype=_F32, **dense)
    cnt, e1, rank, e2 = _peer_select(q_p, sub_keys, tm=_Blocks.peer_select_tokens)
    y = _peer_experts(h2_bf_t, peer_u.astype(_BF16), peer_v.astype(_BF16).T,
                      cnt, e1, rank, e2, h2, ln3_g.reshape(1, -1), ln3_b.reshape(1, -1),
                      tm=_Blocks.peer_expert_tokens)
    return y.reshape(batch, seq, d)


def kernel(x, mem, w_in, gla_gate_up, gla_gate_bias, gla_norm_g, w_out, ln1_g, ln1_b,
           xattn_wq, xattn_wkv, xattn_wo, ln2_g, ln2_b,
           peer_wq, peer_sub_keys, peer_u, peer_v, ln3_g, ln3_b):
    h = x
    for l in range(w_in.shape[0]):
        h = _layer(h, mem, w_in[l], gla_gate_up[l], gla_gate_bias[l], gla_norm_g[l], w_out[l],
                   ln1_g[l], ln1_b[l], xattn_wq[l], xattn_wkv[l], xattn_wo[l], ln2_g[l], ln2_b[l],
                   peer_wq[l], peer_sub_keys[l], peer_u[l], peer_v[l], ln3_g[l], ln3_b[l])
    return h
```

```python
import functools
import math

import jax
import jax.numpy as jnp
import numpy as np
from jax import lax
from jax.experimental import pallas as pl
from jax.experimental.pallas import tpu as pltpu

MOBA_HEADS = 8
MOBA_HEAD_DIM = 128
MOBA_BLOCK = 256
MOBA_TOPK = 3
GLA_HEADS = 4
GLA_KEY_DIM = 128
GLA_VAL_DIM = 256
GLA_GATE_RANK = 16
GLA_GATE_TAU = 16.0
GLA_CHUNK = 64
XATTN_HEADS = 4
PEER_HEADS = 8
PEER_N_KEYS = 128
PEER_TOPK = 16
DEPTH = 1
DEEPNORM_ALPHA = (2.0 * DEPTH) ** 0.25
LN_EPS = 1e-5
RMS_EPS = 1e-6
NEG_INF = -1e30

V7X_LANES = 128
V7X_SUBLANES = 8
V7X_VMEM_LIMIT_BYTES = 56 * 1024 * 1024
V7X_MXU_COLS = 256


class _Blocks:
    matmul_rows = 1024
    matmul_cols = 4 * V7X_MXU_COLS
    proj_cols = 5 * V7X_MXU_COLS
    ln_rows = 512
    xattn_rows = 512
    peer_select_tokens = 2 * V7X_LANES
    peer_expert_tokens = 4 * V7X_LANES

_BF16 = jnp.bfloat16
_F32 = jnp.float32
_NT = (((1,), (1,)), ((), ()))
_TN = (((0,), (0,)), ((), ()))


def _params(*semantics):
    return pltpu.CompilerParams(dimension_semantics=semantics,
                                vmem_limit_bytes=V7X_VMEM_LIMIT_BYTES)


def _matmul_kernel(a_ref, b_ref, o_ref, *maybe_abf_ref):
    if maybe_abf_ref:
        abf_ref, = maybe_abf_ref

        @pl.when(pl.program_id(1) == 0)
        def _():
            abf_ref[...] = a_ref[...].astype(_BF16)

        a = abf_ref[...]
    else:
        a = a_ref[...]
    o_ref[...] = jnp.dot(a, b_ref[...], preferred_element_type=_F32).astype(o_ref.dtype)


def _matmul(a, b, *, bm, bn, out_dtype):
    m, k = a.shape
    _, n = b.shape
    bm = min(bm, m)
    assert m % bm == 0 and n % bn == 0
    cast_in_kernel = a.dtype != _BF16
    return pl.pallas_call(
        _matmul_kernel,
        grid=(m // bm, n // bn),
        in_specs=[pl.BlockSpec((bm, k), lambda i, j: (i, 0)),
                  pl.BlockSpec((k, bn), lambda i, j: (0, j))],
        out_specs=pl.BlockSpec((bm, bn), lambda i, j: (i, j)),
        out_shape=jax.ShapeDtypeStruct((m, n), out_dtype),
        scratch_shapes=[pltpu.VMEM((bm, k), _BF16)] if cast_in_kernel else [],
        compiler_params=_params("parallel", "arbitrary" if cast_in_kernel else "parallel"),
        name="matmul",
    )(a, b)


def _layer_norm_rows(z, g, b):
    mu = jnp.mean(z, axis=-1, keepdims=True)
    zc = z - mu
    var = jnp.mean(zc * zc, axis=-1, keepdims=True)
    return zc * lax.rsqrt(var + LN_EPS) * g + b


def _matmul_residual_ln_kernel(a1_ref, a2_ref, w_ref, r_ref, g_ref, b_ref, o_ref, obf_ref,
                               *maybe_obt_ref):
    kh = a1_ref.shape[1]
    acc = jnp.dot(a1_ref[...], w_ref[:kh, :], preferred_element_type=_F32)
    acc = acc + jnp.dot(a2_ref[...], w_ref[kh:, :], preferred_element_type=_F32)
    y = _layer_norm_rows(DEEPNORM_ALPHA * r_ref[...] + acc, g_ref[...], b_ref[...])
    o_ref[...] = y
    obf_ref[...] = y.astype(_BF16)
    for obt_ref in maybe_obt_ref:
        obt_ref[...] = y.T.astype(_BF16)


def _matmul_residual_ln(a1, a1_col, a2, a2_col, w, resid, g, b, *, bm, transposed_copy=False):
    m, d = resid.shape
    k = w.shape[0]
    kh = k // 2
    out_specs = [pl.BlockSpec((bm, d), lambda i: (i, 0)), pl.BlockSpec((bm, d), lambda i: (i, 0))]
    out_shape = [jax.ShapeDtypeStruct((m, d), _F32), jax.ShapeDtypeStruct((m, d), _BF16)]
    if transposed_copy:
        out_specs.append(pl.BlockSpec((d, bm), lambda i: (0, i)))
        out_shape.append(jax.ShapeDtypeStruct((d, m), _BF16))
    return pl.pallas_call(
        _matmul_residual_ln_kernel,
        grid=(m // bm,),
        in_specs=[pl.BlockSpec((bm, kh), lambda i: (i, a1_col)),
                  pl.BlockSpec((bm, kh), lambda i: (i, a2_col)),
                  pl.BlockSpec((k, d), lambda i: (0, 0)),
                  pl.BlockSpec((bm, d), lambda i: (i, 0)),
                  pl.BlockSpec((1, d), lambda i: (0, 0)),
                  pl.BlockSpec((1, d), lambda i: (0, 0))],
        out_specs=out_specs,
        out_shape=out_shape,
        compiler_params=_params("parallel"),
        name="matmul_residual_ln",
    )(a1, a2, w, resid, g, b)


def _moba_select_blocks(q_ref, k_ref, sel_ref, n_blocks):
    bs = MOBA_BLOCK
    seq, dh = q_ref.shape
    kmean = jnp.sum(k_ref[...].reshape(n_blocks, bs, dh), axis=1) * (1.0 / bs)
    gate = lax.dot_general(kmean, q_ref[...], _NT, precision=lax.Precision.HIGHEST,
                           preferred_element_type=_F32)
    blk = lax.broadcasted_iota(jnp.int32, gate.shape, 0)
    tok = lax.broadcasted_iota(jnp.int32, gate.shape, 1)
    gate = jnp.where(tok >= (blk + 1) * bs, gate, NEG_INF)
    rank = jnp.zeros(gate.shape, _F32)
    for j in range(n_blocks):
        gj = gate[j:j + 1, :]
        ahead = jnp.logical_or(gj > gate, jnp.logical_and(gj == gate, blk > j))
        rank = rank + jnp.where(ahead, 1.0, 0.0)
    picked = jnp.logical_and(rank < MOBA_TOPK, gate > 0.5 * NEG_INF)
    sel_t = jnp.where(picked, 1.0, 0.0)
    sel_t = jnp.concatenate([sel_t, jnp.zeros((V7X_LANES - n_blocks, seq), _F32)], axis=0)
    sel_ref[...] = sel_t.T


def _moba_block_body(q_ref, kb_ref, vb_ref, o_ref, sel_ref, n_past):
    bs = MOBA_BLOCK
    dh = q_ref.shape[1]
    n_keys = (n_past + 1) * bs
    own = slice(n_past * bs, n_keys)
    qb = (q_ref[own, :] * (dh ** -0.5)).astype(_BF16)
    row = lax.broadcasted_iota(jnp.int32, (bs, bs), 0)
    col = lax.broadcasted_iota(jnp.int32, (bs, bs), 1)
    parts = []
    if n_past > 0:
        sel = sel_ref[own, :]
        for j in range(n_past):
            s = lax.dot_general(qb, kb_ref[j * bs:(j + 1) * bs, :], _NT,
                                preferred_element_type=_F32)
            parts.append(jnp.where(sel[:, j:j + 1] > 0.0, s, NEG_INF))
    s = lax.dot_general(qb, kb_ref[own, :], _NT, preferred_element_type=_F32)
    parts.append(jnp.where(col <= row, s, NEG_INF))
    s_all = jnp.concatenate(parts, axis=-1)
    m = jnp.max(s_all, axis=-1, keepdims=True)
    p = jnp.exp(s_all - m)
    l = jnp.sum(p, axis=-1, keepdims=True)
    o = jnp.dot(p.astype(_BF16), vb_ref[0:n_keys, :], preferred_element_type=_F32)
    o_ref[own, :] = (o / l).astype(o_ref.dtype)


def _moba_kernel(q_ref, k_ref, v_ref, o_ref, sel_ref, kb_ref, vb_ref, *, n_blocks):
    if n_blocks > 1:
        _moba_select_blocks(q_ref, k_ref, sel_ref, n_blocks)
    kb_ref[...] = k_ref[...].astype(_BF16)
    vb_ref[...] = v_ref[...].astype(_BF16)
    for c in range(n_blocks):
        _moba_block_body(q_ref, kb_ref, vb_ref, o_ref, sel_ref, c)


def _moba(proj, *, batch, seq, q_col, k_col, v_col):
    nb = seq // MOBA_BLOCK
    dh = MOBA_HEAD_DIM
    return pl.pallas_call(
        functools.partial(_moba_kernel, n_blocks=nb),
        grid=(batch, MOBA_HEADS),
        in_specs=[pl.BlockSpec((seq, dh), lambda b, h: (b, q_col + h)),
                  pl.BlockSpec((seq, dh), lambda b, h: (b, k_col + h)),
                  pl.BlockSpec((seq, dh), lambda b, h: (b, v_col + h))],
        out_specs=pl.BlockSpec((seq, dh), lambda b, h: (b, h)),
        out_shape=jax.ShapeDtypeStruct((batch * seq, MOBA_HEADS * dh), _BF16),
        scratch_shapes=[pltpu.VMEM((seq, V7X_LANES), _F32),
                        pltpu.VMEM((seq, dh), _BF16),
                        pltpu.VMEM((seq, dh), _BF16)],
        compiler_params=_params("parallel", "parallel"),
        name="moba",
    )(proj, proj, proj)


GLA_GROUP = 4
GLA_UNROLL = 4


def _log_sigmoid(z):
    return jnp.minimum(z, 0.0) - jnp.log(1.0 + jnp.exp(-jnp.abs(z)))


def _gla_kernel(q_ref, k_ref, v_ref, r_ref, lr_ref, up_ref, bias_ref, g_ref, o_ref,
                la_ref, qd_ref, ks_ref, oi_ref, dec_ref, *, n_chunks):
    c = GLA_CHUNK
    dk = q_ref.shape[1]
    dv = v_ref.shape[1]
    sub = V7X_SUBLANES
    gain = g_ref[...]
    hi = lax.Precision.HIGHEST

    z = jnp.dot(lr_ref[...], up_ref[...], precision=hi, preferred_element_type=_F32)
    la_ref[...] = _log_sigmoid(z + bias_ref[...]) * (1.0 / GLA_GATE_TAU)

    grp = GLA_GROUP
    gr = grp * c
    row = lax.broadcasted_iota(jnp.int32, (gr, gr), 0)
    col = lax.broadcasted_iota(jnp.int32, (gr, gr), 1)
    shift = c.bit_length() - 1
    causal = jnp.logical_and(jnp.right_shift(row, shift) == jnp.right_shift(col, shift), col <= row)
    causal_f = causal.astype(_F32)
    for g in range(n_chunks // grp):
        rows = slice(g * gr, (g + 1) * gr)
        cum = jnp.dot(causal_f, la_ref[rows, :], precision=hi, preferred_element_type=_F32)
        cum_last = jnp.concatenate(
            [jnp.broadcast_to(cum[(u + 1) * c - 1:(u + 1) * c, :], (c, dk)) for u in range(grp)],
            axis=0)
        q = q_ref[rows, :] * (dk ** -0.5)
        k = k_ref[rows, :]
        q_dec = (q * jnp.exp(cum)).astype(_BF16)
        k_dec = (k * jnp.exp(-cum)).astype(_BF16)
        a = lax.dot_general(q_dec, k_dec, _NT, preferred_element_type=_F32)
        a = jnp.where(causal, a, 0.0).astype(_BF16)
        qd_ref[rows, :] = q_dec
        ks_ref[rows, :] = (k * jnp.exp(cum_last - cum)).astype(_BF16)
        oi_ref[rows, :] = jnp.dot(a, v_ref[rows, :].astype(_BF16), preferred_element_type=_F32)
        for u in range(grp):
            n = g * grp + u
            dec_ref[n * sub:(n + 1) * sub, :] = jnp.exp(cum_last[u * c:u * c + sub, :])

    def inter(n, state_t):
        rows = pl.ds(pl.multiple_of(n * c, c), c)
        o = oi_ref[rows, :] + lax.dot_general(qd_ref[rows, :], state_t.astype(_BF16), _NT,
                                              preferred_element_type=_F32)
        delta_t = lax.dot_general(v_ref[rows, :].astype(_BF16), ks_ref[rows, :], _TN,
                                  preferred_element_type=_F32)
        decay = dec_ref[pl.ds(pl.multiple_of(n * sub, sub), 1), :]
        o = o * lax.rsqrt(jnp.mean(o * o, axis=-1, keepdims=True) + RMS_EPS) * gain
        r = r_ref[rows, :]
        o_ref[rows, :] = (o * (r * jax.nn.sigmoid(r))).astype(o_ref.dtype)
        return state_t * decay + delta_t

    lax.fori_loop(0, n_chunks, inter, jnp.zeros((dv, dk), _F32), unroll=GLA_UNROLL)


def _gla(proj, gate_up, gate_bias, norm_g, *, batch, seq, q_col, k_col, v_col, r_col, lr_col):
    dk, dv = GLA_KEY_DIM, GLA_VAL_DIM
    return pl.pallas_call(
        functools.partial(_gla_kernel, n_chunks=seq // GLA_CHUNK),
        grid=(batch, GLA_HEADS),
        in_specs=[pl.BlockSpec((seq, dk), lambda b, h: (b, q_col + h)),
                  pl.BlockSpec((seq, dk), lambda b, h: (b, k_col + h)),
                  pl.BlockSpec((seq, dv), lambda b, h: (b, v_col + h)),
                  pl.BlockSpec((seq, dv), lambda b, h: (b, r_col + h)),
                  pl.BlockSpec((seq, V7X_LANES), lambda b, h: (b, lr_col)),
                  pl.BlockSpec((V7X_LANES, dk), lambda b, h: (0, h)),
                  pl.BlockSpec((1, dk), lambda b, h: (0, h)),
                  pl.BlockSpec((1, dv), lambda b, h: (0, 0))],
        out_specs=pl.BlockSpec((seq, dv), lambda b, h: (b, h)),
        out_shape=jax.ShapeDtypeStruct((batch * seq, GLA_HEADS * dv), _BF16),
        scratch_shapes=[pltpu.VMEM((seq, dk), _F32),
                        pltpu.VMEM((seq, dk), _BF16),
                        pltpu.VMEM((seq, dk), _BF16),
                        pltpu.VMEM((seq, dv), _F32),
                        pltpu.VMEM((seq // GLA_CHUNK * V7X_SUBLANES, dk), _F32)],
        compiler_params=_params("parallel", "parallel"),
        name="gla",
    )(proj, proj, proj, proj, proj, gate_up, gate_bias, norm_g)


def _xattn_kernel(q_ref, kv_ref, o_ref):
    d = q_ref.shape[1]
    hd = d // XATTN_HEADS
    scale = hd ** -0.5
    for h in range(XATTN_HEADS):
        qh = q_ref[:, h * hd:(h + 1) * hd]
        kh = kv_ref[:, h * hd:(h + 1) * hd]
        vh = kv_ref[:, d + h * hd:d + (h + 1) * hd]
        s = lax.dot_general(qh, kh, _NT, preferred_element_type=_F32) * scale
        m = jnp.max(s, axis=-1, keepdims=True)
        p = jnp.exp(s - m)
        l = jnp.sum(p, axis=-1, keepdims=True)
        o = jnp.dot(p.astype(_BF16), vh, preferred_element_type=_F32)
        o_ref[:, h * hd:(h + 1) * hd] = (o / l).astype(o_ref.dtype)


def _xattn(q, kv, *, batch, seq, mem_len, bq):
    d = q.shape[1]
    nq = seq // bq
    return pl.pallas_call(
        _xattn_kernel,
        grid=(batch, nq),
        in_specs=[pl.BlockSpec((bq, d), lambda b, i: (b * nq + i, 0)),
                  pl.BlockSpec((mem_len, 2 * d), lambda b, i: (b, 0))],
        out_specs=pl.BlockSpec((bq, d), lambda b, i: (b * nq + i, 0)),
        out_shape=jax.ShapeDtypeStruct((batch * seq, d), _BF16),
        compiler_params=_params("parallel", "parallel"),
        name="xattn",
    )(q, kv)


def _sorting_network(n):
    pairs = []
    p = 1
    while p < n:
        k = p
        while k >= 1:
            for j in range(k % p, n - k, 2 * k):
                for i in range(min(k, n - j - k)):
                    if (i + j) // (2 * p) == (i + j + k) // (2 * p):
                        pairs.append((i + j, i + j + k))
            k //= 2
        p *= 2
    return pairs


def _compare_exchange(c, i, j):
    c[i], c[j] = jnp.maximum(c[i], c[j]), jnp.minimum(c[i], c[j])


def _top16_sorted(groups):
    k = len(groups)
    c = list(groups)
    for i, j in _sorting_network(k):
        _compare_exchange(c, i, j)
    shift = V7X_SUBLANES // 2
    while shift >= 1:
        p = [pltpu.roll(x, shift, axis=0) for x in c]
        c = [jnp.maximum(c[i], p[k - 1 - i]) for i in range(k)]
        d = k // 2
        while d >= 1:
            for i in range(k):
                if not i & d:
                    _compare_exchange(c, i, i + d)
            d //= 2
        shift //= 2
    return c


def _prefix_count(vals, pred):
    g1 = pred(vals[7])
    g2 = pred(jnp.where(g1, vals[11], vals[3]))
    g3 = pred(jnp.where(g1, jnp.where(g2, vals[13], vals[9]), jnp.where(g2, vals[5], vals[1])))
    lo = jnp.where(g2, jnp.where(g3, vals[6], vals[4]), jnp.where(g3, vals[2], vals[0]))
    hi = jnp.where(g2, jnp.where(g3, vals[14], vals[12]), jnp.where(g3, vals[10], vals[8]))
    g4 = pred(jnp.where(g1, hi, lo))
    g5 = pred(vals[15])
    return (jnp.where(g1, 8.0, 0.0) + jnp.where(g2, 4.0, 0.0) + jnp.where(g3, 2.0, 0.0)
            + jnp.where(g4, 1.0, 0.0) + jnp.where(g5, 1.0, 0.0))


def _stack_rows(reps):
    sub = lax.broadcasted_iota(jnp.int32, reps[0].shape, 0)
    out = reps[0]
    for d in range(1, len(reps)):
        out = jnp.where(sub == d, reps[d], out)
    return out


def _pair_bf16(x):
    u = pltpu.bitcast(x.astype(_BF16).astype(_F32), jnp.uint32)
    return u | lax.shift_right_logical(u, jnp.uint32(16))


def _topk_exact(s, k):
    rows, tm = s.shape
    row = lax.broadcasted_iota(jnp.int32, s.shape, 0).astype(_F32)
    rid = lax.broadcasted_iota(jnp.int32, (k, tm), 0)
    top = jnp.zeros((k, tm), _F32)
    pos = jnp.full(s.shape, float(k), _F32)
    for r in range(k):
        m = jnp.max(s, axis=0, keepdims=True)
        first = jnp.min(jnp.where(s == m, row, float(rows)), axis=0, keepdims=True)
        hit = row == first
        top = jnp.where(rid == r, m, top)
        pos = jnp.where(hit, float(r), pos)
        s = jnp.where(hit, -jnp.inf, s)
    return top, pos


def _peer_select_exact_head(h, q_ref, keys_ref, cnt_ref, e1_ref, rank_ref, e2_ref):
    kk = PEER_TOPK
    dq = keys_ref.shape[3]
    hi = lax.Precision.HIGHEST
    col = pl.multiple_of(h * (2 * dq), 2 * dq)
    s1 = lax.dot_general(keys_ref[h, 0], q_ref[:, pl.ds(col, dq)], _NT, precision=hi,
                         preferred_element_type=_F32)
    s2 = lax.dot_general(keys_ref[h, 1], q_ref[:, pl.ds(col + dq, dq)], _NT, precision=hi,
                         preferred_element_type=_F32)
    a, pos1 = _topk_exact(s1, kk)
    b, pos2 = _topk_exact(s2, kk)
    cand = jnp.concatenate([a[r:r + 1, :] + b for r in range(kk)], axis=0)
    _, posc = _topk_exact(cand, kk)
    picked = posc < kk
    z = jnp.sum(jnp.where(picked, jnp.exp(cand - (a[0:1, :] + b[0:1, :])), 0.0),
                axis=0, keepdims=True)
    cnt1 = jnp.zeros(s1.shape, _F32)
    for r in range(kk):
        count = jnp.sum(jnp.where(picked[r * kk:(r + 1) * kk, :], 1.0, 0.0), axis=0, keepdims=True)
        cnt1 = jnp.where(pos1 == r, count, cnt1)
    cnt_ref[h] = _pair_bf16(cnt1)
    e1_ref[h] = _pair_bf16(jnp.exp(s1 - a[0:1, :]))
    rank_ref[h] = pos2.astype(_BF16)
    e2_ref[h] = (jnp.exp(s2 - b[0:1, :]) * (0.5 / z)).astype(_BF16)


def _count_at_least(groups, bound):
    n = None
    for g in groups:
        hit = jnp.where(g >= bound, 1.0, 0.0)
        n = hit if n is None else n + hit
    return jnp.sum(n, axis=0, keepdims=True)


def _peer_select_kernel(q_ref, keys_ref, cnt_ref, e1_ref, rank_ref, e2_ref, tied_ref):
    kk = PEER_TOPK
    sub = V7X_SUBLANES
    dq = keys_ref.shape[3]
    n_groups = keys_ref.shape[2] // sub
    assert kk == 16 and n_groups == 16
    hi = lax.Precision.HIGHEST
    for h in range(PEER_HEADS):
        q1 = q_ref[:, (2 * h) * dq:(2 * h + 1) * dq]
        q2 = q_ref[:, (2 * h + 1) * dq:(2 * h + 2) * dq]
        s1 = lax.dot_general(keys_ref[h, 0], q1, _NT, precision=hi,
                             preferred_element_type=_F32)
        s2 = lax.dot_general(keys_ref[h, 1], q2, _NT, precision=hi,
                             preferred_element_type=_F32)
        g1 = [s1[r * sub:(r + 1) * sub, :] for r in range(n_groups)]
        g2 = [s2[r * sub:(r + 1) * sub, :] for r in range(n_groups)]
        a = _top16_sorted(g1)
        b = _top16_sorted(g2)
        a16 = jnp.concatenate([_stack_rows(a[:sub]), _stack_rows(a[sub:])], axis=0)
        b16 = jnp.concatenate([_stack_rows(b[:sub]), _stack_rows(b[sub:])], axis=0)
        half = kk // 2
        cells = [a16[0:1, :] + b16[0:half, :], a16[0:1, :] + b16[half:kk, :]]
        cells += [a16[r:r + 1, :] + b16[0:half, :] for r in range(1, half)]
        cells += [a16[half:kk, :] + b16[0:1, :]]
        lowest = cells[0] + NEG_INF
        best = _top16_sorted(cells + [lowest] * (kk - len(cells)))
        tau = best[kk - 1]
        over = (_count_at_least(g1, a[kk - 1]) + _count_at_least(g2, b[kk - 1])
                + _count_at_least(cells, tau)) != 3.0 * kk
        tied_ref[h:h + 1, :] = jnp.where(over, 1.0, 0.0)
        best16 = jnp.concatenate([_stack_rows(best[:sub]), _stack_rows(best[sub:])], axis=0)
        z = jnp.sum(jnp.exp(best16 - (a16[0:1, :] + b16[0:1, :])), axis=0, keepdims=True)
        inv_z = 0.5 / z
        cnt1, e1, rank2, e2 = [], [], [], []
        for r in range(n_groups):
            x1, x2 = g1[r], g2[r]
            cnt1.append(_prefix_count(b, lambda piv: x1 + piv >= tau))
            rank2.append(_prefix_count(b, lambda piv: piv > x2))
            e1.append(jnp.exp(x1 - a[0]))
            e2.append(jnp.exp(x2 - b[0]) * inv_z)
        cnt_ref[h] = _pair_bf16(jnp.concatenate(cnt1, axis=0))
        e1_ref[h] = _pair_bf16(jnp.concatenate(e1, axis=0))
        rank_ref[h] = jnp.concatenate(rank2, axis=0).astype(_BF16)
        e2_ref[h] = jnp.concatenate(e2, axis=0).astype(_BF16)

    @pl.when(jnp.max(tied_ref[...]) > 0.0)
    def _():
        def redo(h, carry):
            @pl.when(jnp.max(tied_ref[pl.ds(h, 1), :]) > 0.0)
            def _():
                _peer_select_exact_head(h, q_ref, keys_ref, cnt_ref, e1_ref, rank_ref, e2_ref)

            return carry

        lax.fori_loop(0, PEER_HEADS, redo, 0)


def _peer_select(q, sub_keys, *, tm):
    t = q.shape[0]
    nk = PEER_N_KEYS
    spec = pl.BlockSpec((PEER_HEADS, nk, tm), lambda i: (0, 0, i))
    pair = jax.ShapeDtypeStruct((PEER_HEADS, nk, t), jnp.uint32)
    half = jax.ShapeDtypeStruct((PEER_HEADS, nk, t), _BF16)
    return pl.pallas_call(
        _peer_select_kernel,
        grid=(t // tm,),
        in_specs=[pl.BlockSpec((tm, q.shape[1]), lambda i: (i, 0)),
                  pl.BlockSpec(sub_keys.shape, lambda i: (0, 0, 0, 0))],
        out_specs=[spec, spec, spec, spec],
        out_shape=[pair, pair, half, half],
        scratch_shapes=[pltpu.VMEM((PEER_HEADS, tm), _F32)],
        compiler_params=_params("parallel"),
        name="peer_select",
    )(q, sub_keys)


PEER_ROWS_PER_CHUNK = V7X_SUBLANES
PEER_LANE_CHUNK = 2 * V7X_LANES


def _bf16_rows(pair_row):
    tile = jnp.broadcast_to(pair_row, (V7X_SUBLANES, pair_row.shape[1]))
    return pltpu.bitcast(tile, _BF16)[0:V7X_SUBLANES]


def _peer_experts_kernel(xt_ref, u_ref, vt_ref, cnt_ref, e1_ref, rank_ref, e2_ref,
                         r_ref, lng_ref, lnb_ref, o_ref, hdn_ref, g_ref, acc_ref):
    e = pl.program_id(1)
    nk = PEER_N_KEYS
    tm = xt_ref.shape[1]
    sl = V7X_SUBLANES
    lw = PEER_LANE_CHUNK

    @pl.when(e == 0)
    def _():
        acc_ref[...] = jnp.zeros_like(acc_ref)

    hdn_ref[...] = jnp.dot(u_ref[...], xt_ref[...], preferred_element_type=_F32)
    sqrt_half = math.sqrt(0.5)
    for lc in range(tm // lw):
        lanes = slice(lc * lw, (lc + 1) * lw)
        for ii in range(PEER_ROWS_PER_CHUNK):
            w = None
            for h in range(PEER_HEADS):
                cnt = _bf16_rows(cnt_ref[h, ii:ii + 1, lanes])[None]
                e1 = _bf16_rows(e1_ref[h, ii:ii + 1, lanes])[None]
                rank = rank_ref[h, :, lanes].reshape(nk // sl, sl, lw)
                e2 = e2_ref[h, :, lanes].reshape(nk // sl, sl, lw)
                term = jnp.where(rank < cnt, e2, jnp.zeros_like(e2)) * e1
                w = term if w is None else w + term
            hh = hdn_ref[ii * nk:(ii + 1) * nk, lanes]
            act = hh * (1.0 + lax.erf(hh * sqrt_half))
            g_ref[ii * nk:(ii + 1) * nk, lanes] = w.reshape(nk, lw) * act.astype(_BF16)
    acc_ref[...] += jnp.dot(vt_ref[...], g_ref[...], preferred_element_type=_F32)

    @pl.when(e == pl.num_programs(1) - 1)
    def _():
        ff = acc_ref[...].T
        o_ref[...] = _layer_norm_rows(DEEPNORM_ALPHA * r_ref[...] + ff, lng_ref[...], lnb_ref[...])


def _peer_experts(xt, u, vt, cnt, e1, rank, e2, resid, ln_g, ln_b, *, tm):
    d, t = xt.shape
    n_exp = u.shape[0]
    nk = PEER_N_KEYS
    rows = PEER_ROWS_PER_CHUNK
    chunk = rows * nk
    return pl.pallas_call(
        _peer_experts_kernel,
        grid=(t // tm, n_exp // chunk),
        in_specs=[pl.BlockSpec((d, tm), lambda i, e: (0, i)),
                  pl.BlockSpec((chunk, d), lambda i, e: (e, 0)),
                  pl.BlockSpec((d, chunk), lambda i, e: (0, e)),
                  pl.BlockSpec((PEER_HEADS, rows, tm), lambda i, e: (0, e, i)),
                  pl.BlockSpec((PEER_HEADS, rows, tm), lambda i, e: (0, e, i)),
                  pl.BlockSpec((PEER_HEADS, nk, tm), lambda i, e: (0, 0, i)),
                  pl.BlockSpec((PEER_HEADS, nk, tm), lambda i, e: (0, 0, i)),
                  pl.BlockSpec((tm, d), lambda i, e: (i, 0)),
                  pl.BlockSpec((1, d), lambda i, e: (0, 0)),
                  pl.BlockSpec((1, d), lambda i, e: (0, 0))],
        out_specs=pl.BlockSpec((tm, d), lambda i, e: (i, 0)),
        out_shape=jax.ShapeDtypeStruct((t, d), _F32),
        scratch_shapes=[pltpu.VMEM((chunk, tm), _F32),
                        pltpu.VMEM((chunk, tm), _BF16),
                        pltpu.VMEM((d, tm), _F32)],
        compiler_params=_params("parallel", "arbitrary"),
        name="peer_experts",
    )(xt, u, vt, cnt, e1, rank, e2, resid, ln_g, ln_b)


def _layer(x, mem, w_in, gate_up, gate_bias, norm_g, w_out, ln1_g, ln1_b,
           wq, wkv, wo, ln2_g, ln2_b, peer_wq, sub_keys, peer_u, peer_v, ln3_g, ln3_b):
    batch, seq, d = x.shape
    mem_len = mem.shape[1]
    t = batch * seq
    x2 = x.reshape(t, d)

    moba_w = MOBA_HEADS * MOBA_HEAD_DIM
    gla_kw = GLA_HEADS * GLA_KEY_DIM
    gla_vw = GLA_HEADS * GLA_VAL_DIM
    cols = w_in.shape[1]
    cols_pad = -(-cols // _Blocks.proj_cols) * _Blocks.proj_cols
    w_in_p = jnp.pad(w_in, ((0, 0), (0, cols_pad - cols))).astype(_BF16)
    proj = _matmul(x2, w_in_p, bm=_Blocks.matmul_rows, bn=_Blocks.proj_cols, out_dtype=_F32)

    q_m, k_m, v_m = 0, moba_w, 2 * moba_w
    q_g = 3 * moba_w
    k_g = q_g + gla_kw
    v_g = k_g + gla_kw
    r_g = v_g + gla_vw
    lr = r_g + gla_vw
    o_moba = _moba(proj, batch=batch, seq=seq, q_col=q_m // MOBA_HEAD_DIM,
                   k_col=k_m // MOBA_HEAD_DIM, v_col=v_m // MOBA_HEAD_DIM)
    gate_up_p = jnp.pad(gate_up, ((0, V7X_LANES - gate_up.shape[0]), (0, 0)))
    o_gla = _gla(proj, gate_up_p, gate_bias.reshape(1, -1), norm_g.reshape(1, -1),
                 batch=batch, seq=seq, q_col=q_g // GLA_KEY_DIM, k_col=k_g // GLA_KEY_DIM,
                 v_col=v_g // GLA_VAL_DIM, r_col=r_g // GLA_VAL_DIM, lr_col=lr // V7X_LANES)
    h1, h1_bf = _matmul_residual_ln(o_moba, 0, o_gla, 0, w_out.astype(_BF16), x2,
                                    ln1_g.reshape(1, -1), ln1_b.reshape(1, -1),
                                    bm=_Blocks.ln_rows)

    dense = dict(bm=_Blocks.matmul_rows, bn=_Blocks.matmul_cols)
    q_x = _matmul(h1_bf, wq.astype(_BF16), out_dtype=_BF16, **dense)
    kv = _matmul(mem.reshape(batch * mem_len, d), wkv.astype(_BF16), out_dtype=_BF16, **dense)
    o_x = _xattn(q_x, kv, batch=batch, seq=seq, mem_len=mem_len, bq=_Blocks.xattn_rows)
    h2, h2_bf, h2_bf_t = _matmul_residual_ln(
        o_x, 0, o_x, 1, wo.astype(_BF16), h1, ln2_g.reshape(1, -1), ln2_b.reshape(1, -1),
        bm=_Blocks.ln_rows, transposed_copy=True)

    q_p = _matmul(h2_bf, peer_wq.astype(_BF16), out_dtype=_F32, **dense)
    cnt, e1, rank, e2 = _peer_select(q_p, sub_keys, tm=_Blocks.peer_select_tokens)
    y = _peer_experts(h2_bf_t, peer_u.astype(_BF16), peer_v.astype(_BF16).T,
                      cnt, e1, rank, e2, h2, ln3_g.reshape(1, -1), ln3_b.reshape(1, -1),
                      tm=_Blocks.peer_expert_tokens)
    return y.reshape(batch, seq, d)


def kernel(x, mem, w_in, gla_gate_up, gla_gate_bias, gla_norm_g, w_out, ln1_g, ln1_b,
           xattn_wq, xattn_wkv, xattn_wo, ln2_g, ln2_b,
           peer_wq, peer_sub_keys, peer_u, peer_v, ln3_g, ln3_b):
    h = x
    for l in range(w_in.shape[0]):
        h = _layer(h, mem, w_in[l], gla_gate_up[l], gla_gate_bias[l], gla_norm_g[l], w_out[l],
                   ln1_g[l], ln1_b[l], xattn_wq[l], xattn_wkv[l], xattn_wo[l], ln2_g[l], ln2_b[l],
                   peer_wq[l], peer_sub_keys[l], peer_u[l], peer_v[l], ln3_g[l], ln3_b[l])
    return h
```
